```python
import math
import jax, jax.numpy as jnp
from jax import lax
import numpy as np

D_MODEL = 1024
BATCH = 32
SEQ = 2048
DEPTH = 1
DEC_BATCH = 16
DEC_SEQ = 16
PAST_LEN = 4096

CHUNK = 64
SB_HEAD_DIM = 64
SB_DIM = D_MODEL // 2
SB_HEADS = SB_DIM // SB_HEAD_DIM
SB_SCALE = SB_HEAD_DIM ** -0.5
Q_BLOCK = 128
CONV_DIM = D_MODEL // 2
CONV_K = 31
N_GROUPS = 4
EXPERTS_PER_GROUP = 4
N_EXPERTS = N_GROUPS * EXPERTS_PER_GROUP
TOP_K = 2
EXPERT_DFF = D_MODEL // 4
LN_EPS = 1e-5
DEEPNORM_ALPHA = (2.0 * DEPTH) ** 0.25
DEEPNORM_BETA = (8.0 * DEPTH) ** -0.25
IN_DIM = 3 * SB_DIM + 2 * CONV_DIM

kernel_name = "stickbreak_conformer_hmoe_stream_step"


def layer_norm(x, g, b):
    xf = x.astype(jnp.float32)
    mu = jnp.mean(xf, axis=-1, keepdims=True)
    var = jnp.mean(jnp.square(xf - mu), axis=-1, keepdims=True)
    y = (xf - mu) * lax.rsqrt(var + LN_EPS) * g.astype(jnp.float32) + b.astype(jnp.float32)
    return y.astype(x.dtype)


def stick_breaking_attention(q, k, v, n_past):
    tq = q.shape[1]
    outs = []
    for qs in range(0, tq, Q_BLOCK):
        qe = min(qs + Q_BLOCK, tq)
        n_keys = n_past + qe
        qb = q[:, qs:qe]
        kb = k[:, :n_keys]
        vb = v[:, :n_keys]
        z = jnp.einsum('bqhd,bkhd->bhqk', qb, kb).astype(jnp.float32) * SB_SCALE
        q_pos = n_past + qs + jnp.arange(qe - qs)
        k_pos = jnp.arange(n_keys)
        mask = k_pos[None, :] < q_pos[:, None]
        log_keep = jnp.where(mask, jax.nn.log_sigmoid(-z), 0.0)
        log_after = lax.cumsum(log_keep, axis=3, reverse=True) - log_keep
        weights = jnp.where(mask, jnp.exp(jax.nn.log_sigmoid(z) + log_after), 0.0)
        outs.append(jnp.einsum('bhqk,bkhd->bqhd', weights.astype(vb.dtype), vb))
    return jnp.concatenate(outs, axis=1)


def causal_depthwise_conv(u_full, conv_w, conv_b):
    out = lax.conv_general_dilated(
        u_full, conv_w[:, None, :].astype(u_full.dtype), window_strides=(1,),
        padding='VALID', dimension_numbers=('NWC', 'WIO', 'NWC'),
        feature_group_count=CONV_DIM)
    return out + conv_b


def hierarchical_moe(h, w_router_group, b_router_group, w_router_expert, b_router_expert,
                     w_expert_gate, w_expert_up, w_expert_down):
    shp = h.shape
    t = h.reshape(-1, D_MODEL)
    n = t.shape[0]
    g_logits = (t @ w_router_group + b_router_group).astype(jnp.float32)
    g_idx = jnp.argmax(g_logits, axis=-1)
    p_group = jnp.take_along_axis(jax.nn.softmax(g_logits, axis=-1), g_idx[:, None], axis=1)
    e_logits = (t @ w_router_expert + b_router_expert).astype(jnp.float32)
    e_logits = e_logits.reshape(n, N_GROUPS, EXPERTS_PER_GROUP)
    e_sel = jnp.take_along_axis(e_logits, g_idx[:, None, None], axis=1)[:, 0]
    top_v, top_i = lax.top_k(e_sel, TOP_K)
    p_expert = jax.nn.softmax(top_v, axis=-1)
    expert_id = g_idx[:, None] * EXPERTS_PER_GROUP + top_i
    combine = jnp.sum(jax.nn.one_hot(expert_id, N_EXPERTS, dtype=jnp.float32)
                      * (p_group * p_expert)[..., None], axis=1).astype(t.dtype)
    y = jnp.zeros_like(t)
    for e in range(N_EXPERTS):
        hid = jax.nn.silu(t @ w_expert_gate[e]) * (t @ w_expert_up[e])
        y = y + combine[:, e:e + 1] * (hid @ w_expert_down[e])
    return y.reshape(shp)


def encoder_layer(x, k_hist, v_hist, conv_hist,
                  w_in, w_branch_attn, w_branch_conv, conv_w, conv_b, conv_ln_g, conv_ln_b,
                  w_gate, b_gate, w_out, ln1_g, ln1_b,
                  w_router_group, b_router_group, w_router_expert, b_router_expert,
                  w_expert_gate, w_expert_up, w_expert_down, ln2_g, ln2_b):
    bsz, tlen, _ = x.shape
    proj = x @ w_in
    q, k, v, glu_val, glu_gate = jnp.split(
        proj, [SB_DIM, 2 * SB_DIM, 3 * SB_DIM, 3 * SB_DIM + CONV_DIM], axis=-1)
    q = q.reshape(bsz, tlen, SB_HEADS, SB_HEAD_DIM)
    k = k.reshape(bsz, tlen, SB_HEADS, SB_HEAD_DIM)
    v = v.reshape(bsz, tlen, SB_HEADS, SB_HEAD_DIM)
    k_all = jnp.concatenate([k_hist, k], axis=1)
    v_all = jnp.concatenate([v_hist, v], axis=1)
    attn = stick_breaking_attention(q, k_all, v_all, k_hist.shape[1])
    branch_a = attn.reshape(bsz, tlen, SB_DIM) @ w_branch_attn
    u = glu_val * jax.nn.sigmoid(glu_gate)
    u_full = jnp.concatenate([conv_hist, u], axis=1)
    c = causal_depthwise_conv(u_full, conv_w, conv_b)
    c = jax.nn.silu(layer_norm(c, conv_ln_g, conv_ln_b))
    branch_b = c @ w_branch_conv
    gates = jax.nn.sigmoid(x @ w_gate + b_gate)
    g_a, g_b = jnp.split(gates, 2, axis=-1)
    mixed = (g_a * branch_a + g_b * branch_b) @ w_out
    h = layer_norm(DEEPNORM_ALPHA * x + mixed, ln1_g, ln1_b)
    ff = hierarchical_moe(h, w_router_group, b_router_group, w_router_expert, b_router_expert,
                          w_expert_gate, w_expert_up, w_expert_down)
    y = layer_norm(DEEPNORM_ALPHA * h + ff, ln2_g, ln2_b)
    return y, k, v, u_full[:, -(CONV_K - 1):]


def setup_inputs(seed: int = 0) -> dict:
    key = jax.random.key(seed)
    ks = jax.random.split(key, 32)

    def nrm(k, shape, scale):
        return jax.random.normal(k, shape, jnp.float32) * scale

    d_sc = D_MODEL ** -0.5
    beta = DEEPNORM_BETA
    w_in = jnp.concatenate([
        nrm(ks[5], (DEPTH, D_MODEL, 2 * SB_DIM), d_sc),
        nrm(ks[6], (DEPTH, D_MODEL, SB_DIM), d_sc * beta),
        nrm(ks[7], (DEPTH, D_MODEL, CONV_DIM), d_sc * beta),
        nrm(ks[8], (DEPTH, D_MODEL, CONV_DIM), d_sc),
    ], axis=-1)
    return {
        "x_prompt": nrm(ks[0], (BATCH, SEQ, D_MODEL), 1.0),
        "x_sample": nrm(ks[1], (DEC_BATCH, DEC_SEQ, D_MODEL), 1.0),
        "cache_k": nrm(ks[2], (DEPTH, DEC_BATCH, PAST_LEN, SB_HEADS, SB_HEAD_DIM), 1.0),
        "cache_v": nrm(ks[3], (DEPTH, DEC_BATCH, PAST_LEN, SB_HEADS, SB_HEAD_DIM), beta),
        "state_conv": nrm(ks[4], (DEPTH, DEC_BATCH, CONV_K - 1, CONV_DIM), 0.5 * beta),
        "w_in": w_in,
        "w_branch_attn": nrm(ks[9], (DEPTH, SB_DIM, D_MODEL), SB_DIM ** -0.5 * beta),
        "w_branch_conv": nrm(ks[10], (DEPTH, CONV_DIM, D_MODEL), CONV_DIM ** -0.5 * beta),
        "conv_w": nrm(ks[11], (DEPTH, CONV_K, CONV_DIM), CONV_K ** -0.5),
        "conv_b": nrm(ks[12], (DEPTH, CONV_DIM), 0.02),
        "conv_ln_g": 1.0 + nrm(ks[13], (DEPTH, CONV_DIM), 0.05),
        "conv_ln_b": nrm(ks[14], (DEPTH, CONV_DIM), 0.02),
        "w_gate": nrm(ks[15], (DEPTH, D_MODEL, 2 * D_MODEL), d_sc),
        "b_gate": nrm(ks[16], (DEPTH, 2 * D_MODEL), 0.02),
        "w_out": nrm(ks[17], (DEPTH, D_MODEL, D_MODEL), d_sc * beta),
        "ln1_g": 1.0 + nrm(ks[18], (DEPTH, D_MODEL), 0.05),
        "ln1_b": nrm(ks[19], (DEPTH, D_MODEL), 0.02),
        "w_router_group": nrm(ks[20], (DEPTH, D_MODEL, N_GROUPS), d_sc),
        "b_router_group": nrm(ks[21], (DEPTH, N_GROUPS), 0.01),
        "w_router_expert": nrm(ks[22], (DEPTH, D_MODEL, N_EXPERTS), d_sc),
        "b_router_expert": nrm(ks[23], (DEPTH, N_EXPERTS), 0.01),
        "w_expert_gate": nrm(ks[24], (DEPTH, N_EXPERTS, D_MODEL, EXPERT_DFF), d_sc * beta),
        "w_expert_up": nrm(ks[25], (DEPTH, N_EXPERTS, D_MODEL, EXPERT_DFF), d_sc * beta),
        "w_expert_down": nrm(ks[26], (DEPTH, N_EXPERTS, EXPERT_DFF, D_MODEL), EXPERT_DFF ** -0.5 * beta),
        "ln2_g": 1.0 + nrm(ks[27], (DEPTH, D_MODEL), 0.05),
        "ln2_b": nrm(ks[28], (DEPTH, D_MODEL), 0.02),
    }


def reference(x_prompt, x_sample, cache_k, cache_v, state_conv,
              w_in, w_branch_attn, w_branch_conv, conv_w, conv_b, conv_ln_g, conv_ln_b,
              w_gate, b_gate, w_out, ln1_g, ln1_b,
              w_router_group, b_router_group, w_router_expert, b_router_expert,
              w_expert_gate, w_expert_up, w_expert_down, ln2_g, ln2_b):
    h_p = x_prompt
    h_s = x_sample
    k_p_list, v_p_list, c_p_list = [], [], []
    k_s_list, v_s_list, c_s_list = [], [], []
    bsz = x_prompt.shape[0]
    for l in range(DEPTH):
        lw = (w_in[l], w_branch_attn[l], w_branch_conv[l], conv_w[l], conv_b[l],
              conv_ln_g[l], conv_ln_b[l], w_gate[l], b_gate[l], w_out[l], ln1_g[l], ln1_b[l],
              w_router_group[l], b_router_group[l], w_router_expert[l], b_router_expert[l],
              w_expert_gate[l], w_expert_up[l], w_expert_down[l], ln2_g[l], ln2_b[l])
        empty_kv = jnp.zeros((bsz, 0, SB_HEADS, SB_HEAD_DIM), h_p.dtype)
        conv_pad = jnp.zeros((bsz, CONV_K - 1, CONV_DIM), h_p.dtype)
        h_p, k_p, v_p, c_p = encoder_layer(h_p, empty_kv, empty_kv, conv_pad, *lw)
        h_s, k_s, v_s, c_s = encoder_layer(h_s, cache_k[l], cache_v[l], state_conv[l], *lw)
        k_p_list.append(k_p)
        v_p_list.append(v_p)
        c_p_list.append(c_p)
        k_s_list.append(k_s)
        v_s_list.append(v_s)
        c_s_list.append(c_s)
    return (h_p, h_s,
            jnp.stack(k_p_list), jnp.stack(v_p_list), jnp.stack(c_p_list),
            jnp.stack(k_s_list), jnp.stack(v_s_list), jnp.stack(c_s_list))
```

```python
import functools

import jax
import jax.numpy as jnp
from jax import lax
from jax.experimental import pallas as pl
from jax.experimental.pallas import tpu as pltpu

F32 = jnp.float32
BF16 = jnp.bfloat16

D_MODEL = 1024
SB_DIM = 512
SB_HEADS = 8
SB_HEAD_DIM = 64
SB_SCALE = SB_HEAD_DIM ** -0.5
CONV_DIM = 512
CONV_K = 31
N_GROUPS = 4
EXPERTS_PER_GROUP = 4
N_EXPERTS = 16
EXPERT_DFF = 256
LN_EPS = 1e-5
IN_DIM = 3 * SB_DIM + 2 * CONV_DIM

LANES = 128
KEY_TILE = 128
HALO = 32
ROUTER_LANES = 128
VMEM_LIMIT = 56 * 1024 * 1024


def _layer_norm(x, g, b):
    mu = jnp.mean(x, axis=-1, keepdims=True)
    xc = x - mu
    var = jnp.mean(xc * xc, axis=-1, keepdims=True)
    return xc * lax.rsqrt(var + LN_EPS) * g + b


def _proj_kernel(x_ref, w_ref, q_ref, k_ref, v_ref, kb_ref, vb_ref, u_ref):
    xb = x_ref[...].astype(BF16)

    def mm(lo, hi):
        return jnp.dot(xb, w_ref[:, lo:hi], preferred_element_type=F32)

    q_ref[...] = (mm(0, SB_DIM) * SB_SCALE).astype(BF16)
    k = mm(SB_DIM, 2 * SB_DIM)
    k_ref[...] = k
    kb_ref[...] = k.astype(BF16)
    v = mm(2 * SB_DIM, 3 * SB_DIM)
    v_ref[...] = v
    vb_ref[...] = v.astype(BF16)
    val = mm(3 * SB_DIM, 3 * SB_DIM + CONV_DIM)
    gate = mm(3 * SB_DIM + CONV_DIM, IN_DIM)
    u_ref[...] = val * jax.nn.sigmoid(gate)


def _proj(x2d, w_in_bf, tm):
    n = x2d.shape[0]
    row = lambda i: (i, 0)
    outs = [jax.ShapeDtypeStruct((n, SB_DIM), BF16),
            jax.ShapeDtypeStruct((n, SB_DIM), F32),
            jax.ShapeDtypeStruct((n, SB_DIM), F32),
            jax.ShapeDtypeStruct((n, SB_DIM), BF16),
            jax.ShapeDtypeStruct((n, SB_DIM), BF16),
            jax.ShapeDtypeStruct((n, CONV_DIM), F32)]
    return pl.pallas_call(
        _proj_kernel,
        grid=(n // tm,),
        in_specs=[pl.BlockSpec((tm, D_MODEL), row),
                  pl.BlockSpec((D_MODEL, IN_DIM), lambda i: (0, 0))],
        out_specs=[pl.BlockSpec((tm, SB_DIM), row)] * 5 + [pl.BlockSpec((tm, CONV_DIM), row)],
        out_shape=outs,
        compiler_params=pltpu.CompilerParams(dimension_semantics=("arbitrary",),
                                             vmem_limit_bytes=VMEM_LIMIT),
        name="proj",
    )(x2d, w_in_bf)


def _attn_kernel(*refs, tq, pairs, n_cache_tiles):
    if n_cache_tiles:
        q_ref, kn_ref, vn_ref, kc_ref, vc_ref, uu_ref, o_ref = refs
    else:
        q_ref, kn_ref, vn_ref, uu_ref, o_ref = refs
    i = pl.program_id(2)
    rows = 2 * tq
    uu = uu_ref[...]

    lane_q = lax.broadcasted_iota(jnp.int32, (tq, LANES), 1)
    lane_k = lax.broadcasted_iota(jnp.int32, (KEY_TILE, LANES), 1)
    r_idx = lax.broadcasted_iota(jnp.int32, (rows, KEY_TILE), 0)
    r_idx = jnp.where(r_idx >= tq, r_idx - tq, r_idx)
    c_idx = lax.broadcasted_iota(jnp.int32, (rows, KEY_TILE), 1)
    causal = c_idx < r_idx

    def tile(qs, kt, vt, carry, mask):
        acc, run = carry
        z = lax.dot_general(qs, kt, (((1,), (1,)), ((), ())), preferred_element_type=F32)
        lk = -(jnp.maximum(z, 0.0) + jnp.log(1.0 + jnp.exp(-jnp.abs(z))))
        if mask is not None:
            lk = jnp.where(mask, lk, 0.0)
        hi = lk.astype(BF16)
        lo = (lk - hi.astype(F32)).astype(BF16)
        su = jnp.dot(jnp.concatenate([hi, lo], axis=1), uu, preferred_element_type=F32)
        s = su[:, :KEY_TILE] + run
        w = jnp.exp(z + s)
        if mask is not None:
            w = jnp.where(mask, w, 0.0)
        wb = w.astype(BF16)
        v0 = jnp.where(lane_k < SB_HEAD_DIM, vt, jnp.zeros_like(vt))
        v1 = jnp.where(lane_k >= SB_HEAD_DIM, vt, jnp.zeros_like(vt))
        acc = acc + jnp.dot(wb[:tq], v0, preferred_element_type=F32)
        acc = acc + jnp.dot(wb[tq:], v1, preferred_element_type=F32)
        return acc, run + su[:, KEY_TILE:]

    for p in range(pairs):
        cols = slice(p * LANES, (p + 1) * LANES)
        qp = q_ref[:, cols]
        qs = jnp.concatenate([jnp.where(lane_q < SB_HEAD_DIM, qp, jnp.zeros_like(qp)),
                              jnp.where(lane_q >= SB_HEAD_DIM, qp, jnp.zeros_like(qp))], axis=0)
        carry = (jnp.zeros((tq, LANES), F32), jnp.zeros((rows, KEY_TILE), F32))
        d0 = pl.multiple_of(i * KEY_TILE, KEY_TILE)
        carry = tile(qs, kn_ref[pl.ds(d0, KEY_TILE), cols], vn_ref[pl.ds(d0, KEY_TILE), cols],
                     carry, causal)

        def new_body(jj, c):
            off = pl.multiple_of((i - 1 - jj) * KEY_TILE, KEY_TILE)
            return tile(qs, kn_ref[pl.ds(off, KEY_TILE), cols], vn_ref[pl.ds(off, KEY_TILE), cols],
                        c, None)

        carry = lax.fori_loop(0, i, new_body, carry)

        if n_cache_tiles:
            def cache_body(jj, c):
                off = pl.multiple_of((n_cache_tiles - 1 - jj) * KEY_TILE, KEY_TILE)
                kt = kc_ref[pl.ds(off, KEY_TILE), cols].astype(BF16)
                vt = vc_ref[pl.ds(off, KEY_TILE), cols].astype(BF16)
                return tile(qs, kt, vt, c, None)

            carry = lax.fori_loop(0, n_cache_tiles, cache_body, carry)

        o_ref[:, cols] = carry[0].astype(BF16)


def _suffix_sum_matrix():
    r = lax.broadcasted_iota(jnp.int32, (2 * KEY_TILE, 2 * KEY_TILE), 0) % KEY_TILE
    c = lax.broadcasted_iota(jnp.int32, (2 * KEY_TILE, 2 * KEY_TILE), 1)
    return jnp.where((c >= KEY_TILE) | (r >= c), 1.0, 0.0).astype(BF16)


def _attention(q, kn, vn, cache_k, cache_v, *, bsz, tq, nq, new_rows, pairs):
    n = q.shape[0]
    width = pairs * LANES
    steps = SB_DIM // width
    n_cache_tiles = 0 if cache_k is None else cache_k.shape[0] // bsz // KEY_TILE
    in_specs = [pl.BlockSpec((tq, width), lambda b, p, i: (b * nq + i, p)),
                pl.BlockSpec((new_rows, width), lambda b, p, i: (b, p)),
                pl.BlockSpec((new_rows, width), lambda b, p, i: (b, p))]
    args = [q, kn, vn]
    if n_cache_tiles:
        past = n_cache_tiles * KEY_TILE
        in_specs += [pl.BlockSpec((past, width), lambda b, p, i: (b, p))] * 2
        args += [cache_k, cache_v]
    in_specs.append(pl.BlockSpec((2 * KEY_TILE, 2 * KEY_TILE), lambda b, p, i: (0, 0)))
    args.append(_suffix_sum_matrix())
    return pl.pallas_call(
        functools.partial(_attn_kernel, tq=tq, pairs=pairs, n_cache_tiles=n_cache_tiles),
        grid=(bsz, steps, nq),
        in_specs=in_specs,
        out_specs=pl.BlockSpec((tq, width), lambda b, p, i: (b * nq + i, p)),
        out_shape=jax.ShapeDtypeStruct((n, SB_DIM), BF16),
        compiler_params=pltpu.CompilerParams(
            dimension_semantics=("arbitrary", "arbitrary", "arbitrary"),
            vmem_limit_bytes=VMEM_LIMIT),
        name="attn",
    )(*args)


def _post_kernel(x_ref, attn_ref, u_ref, halo_ref, hist_ref, wba_ref, wbc_ref, cw_ref, cb_ref,
                 clg_ref, clb_ref, wg_ref, bg_ref, wo_ref, l1g_ref, l1b_ref, wr_ref, br_ref,
                 h_ref, comb_ref, ext_ref, *, tm, alpha):
    i = pl.program_id(1)
    prev = jnp.where(i == 0, hist_ref[...], halo_ref[...])
    ext_ref[0:HALO, :] = prev
    ext_ref[HALO:HALO + tm, :] = u_ref[...]
    first = HALO - (CONV_K - 1)
    c = jnp.zeros((tm, CONV_DIM), F32) + cb_ref[...]
    for j in range(CONV_K):
        c = c + ext_ref[first + j:first + j + tm, :] * cw_ref[j:j + 1, :]
    c = _layer_norm(c, clg_ref[...], clb_ref[...])
    c = c * jax.nn.sigmoid(c)
    branch_b = jnp.dot(c.astype(BF16), wbc_ref[...], preferred_element_type=F32)
    branch_a = jnp.dot(attn_ref[...], wba_ref[...], preferred_element_type=F32)
    x = x_ref[...]
    gates = jax.nn.sigmoid(jnp.dot(x.astype(BF16), wg_ref[...], preferred_element_type=F32)
                           + bg_ref[...])
    m = gates[:, :D_MODEL] * branch_a + gates[:, D_MODEL:] * branch_b
    mixed = jnp.dot(m.astype(BF16), wo_ref[...], preferred_element_type=F32)
    h = _layer_norm(alpha * x + mixed, l1g_ref[...], l1b_ref[...])
    h_ref[...] = h

    logits = jnp.dot(h.astype(BF16), wr_ref[...], preferred_element_type=F32) + br_ref[...]
    lane = lax.broadcasted_iota(jnp.int32, (tm, ROUTER_LANES), 1)
    neg = jnp.float32(-jnp.inf)
    gl = jnp.where(lane < N_GROUPS, logits, neg)
    gmax = jnp.max(gl, axis=-1, keepdims=True)
    g_idx = jnp.min(jnp.where(gl == gmax, lane, ROUTER_LANES), axis=-1, keepdims=True)
    p_group = 1.0 / jnp.sum(jnp.exp(gl - gmax), axis=-1, keepdims=True)
    lo = N_GROUPS + EXPERTS_PER_GROUP * g_idx
    el = jnp.where((lane >= lo) & (lane < lo + EXPERTS_PER_GROUP), logits, neg)
    top1 = jnp.max(el, axis=-1, keepdims=True)
    i1 = jnp.min(jnp.where(el == top1, lane, ROUTER_LANES), axis=-1, keepdims=True)
    el2 = jnp.where(lane == i1, neg, el)
    top2 = jnp.max(el2, axis=-1, keepdims=True)
    i2 = jnp.min(jnp.where(el2 == top2, lane, ROUTER_LANES), axis=-1, keepdims=True)
    e2 = jnp.exp(top2 - top1)
    p1 = 1.0 / (1.0 + e2)
    p2 = e2 / (1.0 + e2)
    comb_ref[...] = (jnp.where(lane == i1, p_group * p1, 0.0)
                     + jnp.where(lane == i2, p_group * p2, 0.0))


def _post(x2d, attn, u, hist, lw, *, bsz, tlen, tm, alpha):
    n = x2d.shape[0]
    nt = tlen // tm
    row = lambda b, i: (b * nt + i, 0)
    const = lambda b, i: (0, 0)

    def halo_map(b, i):
        return (jnp.maximum((b * tlen + i * tm) // HALO - 1, 0), 0)

    full = lambda a: pl.BlockSpec(a.shape, const)
    weights = [lw["wba"], lw["wbc"], lw["conv_w"], lw["conv_b"], lw["conv_ln_g"], lw["conv_ln_b"],
               lw["w_gate"], lw["b_gate"], lw["w_out"], lw["ln1_g"], lw["ln1_b"],
               lw["w_router"], lw["b_router"]]
    return pl.pallas_call(
        functools.partial(_post_kernel, tm=tm, alpha=alpha),
        grid=(bsz, nt),
        in_specs=[pl.BlockSpec((tm, D_MODEL), row),
                  pl.BlockSpec((tm, SB_DIM), row),
                  pl.BlockSpec((tm, CONV_DIM), row),
                  pl.BlockSpec((HALO, CONV_DIM), halo_map),
                  pl.BlockSpec((HALO, CONV_DIM), lambda b, i: (b, 0))] + [full(w) for w in weights],
        out_specs=[pl.BlockSpec((tm, D_MODEL), row), pl.BlockSpec((tm, ROUTER_LANES), row)],
        out_shape=[jax.ShapeDtypeStruct((n, D_MODEL), F32),
                   jax.ShapeDtypeStruct((n, ROUTER_LANES), F32)],
        scratch_shapes=[pltpu.VMEM((HALO + tm, CONV_DIM), F32)],
        compiler_params=pltpu.CompilerParams(dimension_semantics=("arbitrary", "arbitrary"),
                                             vmem_limit_bytes=VMEM_LIMIT),
        name="post",
    )(x2d, attn, u, u, hist, *weights)


def _moe_kernel(h_ref, comb_ref, wg_ref, wu_ref, wd_ref, l2g_ref, l2b_ref, y_ref,
                hb_ref, acc_ref, *, tm, alpha):
    e = pl.program_id(1)

    @pl.when(e == 0)
    def _():
        hb_ref[...] = h_ref[...].astype(BF16)
        acc_ref[...] = jnp.zeros_like(acc_ref)

    hb = hb_ref[...]
    gate = jnp.dot(hb, wg_ref[0], preferred_element_type=F32)
    up = jnp.dot(hb, wu_ref[0], preferred_element_type=F32)
    lane = lax.broadcasted_iota(jnp.int32, (tm, ROUTER_LANES), 1)
    c_e = jnp.sum(jnp.where(lane == N_GROUPS + e, comb_ref[...], 0.0), axis=-1, keepdims=True)
    hid = gate * jax.nn.sigmoid(gate) * up * c_e
    acc_ref[...] += jnp.dot(hid.astype(BF16), wd_ref[0], preferred_element_type=F32)

    @pl.when(e == N_EXPERTS - 1)
    def _():
        y_ref[...] = _layer_norm(alpha * h_ref[...] + acc_ref[...], l2g_ref[...], l2b_ref[...])


def _moe(h, comb, lw, *, tm, alpha):
    n = h.shape[0]
    row = lambda i, e: (i, 0)
    const = lambda i, e: (0, 0)
    return pl.pallas_call(
        functools.partial(_moe_kernel, tm=tm, alpha=alpha),
        grid=(n // tm, N_EXPERTS),
        in_specs=[pl.BlockSpec((tm, D_MODEL), row),
                  pl.BlockSpec((tm, ROUTER_LANES), row),
                  pl.BlockSpec((1, D_MODEL, EXPERT_DFF), lambda i, e: (e, 0, 0)),
                  pl.BlockSpec((1, D_MODEL, EXPERT_DFF), lambda i, e: (e, 0, 0)),
                  pl.BlockSpec((1, EXPERT_DFF, D_MODEL), lambda i, e: (e, 0, 0)),
                  pl.BlockSpec((1, D_MODEL), const),
                  pl.BlockSpec((1, D_MODEL), const)],
        out_specs=pl.BlockSpec((tm, D_MODEL), row),
        out_shape=jax.ShapeDtypeStruct((n, D_MODEL), F32),
        scratch_shapes=[pltpu.VMEM((tm, D_MODEL), BF16), pltpu.VMEM((tm, D_MODEL), F32)],
        compiler_params=pltpu.CompilerParams(dimension_semantics=("arbitrary", "arbitrary"),
                                             vmem_limit_bytes=VMEM_LIMIT),
        name="moe",
    )(h, comb, lw["w_expert_gate"], lw["w_expert_up"], lw["w_expert_down"],
      lw["ln2_g"], lw["ln2_b"])


def _encoder_layer(x, k_hist, v_hist, conv_hist, lw, *, alpha, proj_tm, post_tm, moe_tm, tq, pairs):
    bsz, tlen, _ = x.shape
    n = bsz * tlen
    x2d = x.reshape(n, D_MODEL)
    q, k, v, kb, vb, u = _proj(x2d, lw["w_in"], proj_tm)

    nq = tlen // tq
    if k_hist is None:
        kn, vn, new_rows, ck, cv = kb, vb, tlen, None, None
    else:
        pad = lambda a: jnp.pad(a.reshape(bsz, tlen, SB_DIM),
                                ((0, 0), (0, KEY_TILE - tlen), (0, 0))).reshape(bsz * KEY_TILE, SB_DIM)
        kn, vn, new_rows = pad(kb), pad(vb), KEY_TILE
        past = k_hist.shape[1]
        ck = k_hist.reshape(bsz * past, SB_DIM)
        cv = v_hist.reshape(bsz * past, SB_DIM)
    attn = _attention(q, kn, vn, ck, cv, bsz=bsz, tq=tq, nq=nq, new_rows=new_rows, pairs=pairs)

    if conv_hist is None:
        hist = jnp.zeros((bsz * HALO, CONV_DIM), F32)
    else:
        hist = jnp.pad(conv_hist, ((0, 0), (HALO - (CONV_K - 1), 0), (0, 0))).reshape(bsz * HALO, CONV_DIM)
    h, comb = _post(x2d, attn, u, hist, lw, bsz=bsz, tlen=tlen, tm=post_tm, alpha=alpha)
    y = _moe(h, comb, lw, tm=moe_tm, alpha=alpha)

    u3 = u.reshape(bsz, tlen, CONV_DIM)
    if conv_hist is None:
        conv_state = u3[:, tlen - (CONV_K - 1):]
    else:
        conv_state = jnp.concatenate([conv_hist, u3], axis=1)[:, -(CONV_K - 1):]
    return (y.reshape(bsz, tlen, D_MODEL),
            k.reshape(bsz, tlen, SB_HEADS, SB_HEAD_DIM),
            v.reshape(bsz, tlen, SB_HEADS, SB_HEAD_DIM),
            conv_state)


def _layer_weights(l, w_in, w_branch_attn, w_branch_conv, conv_w, conv_b, conv_ln_g, conv_ln_b,
                   w_gate, b_gate, w_out, ln1_g, ln1_b, w_router_group, b_router_group,
                   w_router_expert, b_router_expert, w_expert_gate, w_expert_up, w_expert_down,
                   ln2_g, ln2_b):
    n_route = N_GROUPS + N_EXPERTS
    w_router = jnp.concatenate([w_router_group[l], w_router_expert[l]], axis=1)
    w_router = jnp.pad(w_router, ((0, 0), (0, ROUTER_LANES - n_route)))
    b_router = jnp.pad(jnp.concatenate([b_router_group[l], b_router_expert[l]]),
                       (0, ROUTER_LANES - n_route))
    row = lambda a: a.reshape(1, -1).astype(F32)
    return dict(
        w_in=w_in[l].astype(BF16), wba=w_branch_attn[l].astype(BF16),
        wbc=w_branch_conv[l].astype(BF16), conv_w=conv_w[l].astype(F32), conv_b=row(conv_b[l]),
        conv_ln_g=row(conv_ln_g[l]), conv_ln_b=row(conv_ln_b[l]),
        w_gate=w_gate[l].astype(BF16), b_gate=row(b_gate[l]), w_out=w_out[l].astype(BF16),
        ln1_g=row(ln1_g[l]), ln1_b=row(ln1_b[l]),
        w_router=w_router.astype(BF16), b_router=row(b_router),
        w_expert_gate=w_expert_gate[l].astype(BF16), w_expert_up=w_expert_up[l].astype(BF16),
        w_expert_down=w_expert_down[l].astype(BF16), ln2_g=row(ln2_g[l]), ln2_b=row(ln2_b[l]))


def kernel(x_prompt, x_sample, cache_k, cache_v, state_conv, w_in, w_branch_attn, w_branch_conv, conv_w, conv_b, conv_ln_g, conv_ln_b, w_gate, b_gate, w_out, ln1_g, ln1_b, w_router_group, b_router_group, w_router_expert, b_router_expert, w_expert_gate, w_expert_up, w_expert_down, ln2_g, ln2_b):
    depth = w_in.shape[0]
    alpha = (2.0 * depth) ** 0.25
    h_p, h_s = x_prompt, x_sample
    outs = [[] for _ in range(6)]
    for l in range(depth):
        lw = _layer_weights(l, w_in, w_branch_attn, w_branch_conv, conv_w, conv_b, conv_ln_g,
                            conv_ln_b, w_gate, b_gate, w_out, ln1_g, ln1_b, w_router_group,
                            b_router_group, w_router_expert, b_router_expert, w_expert_gate,
                            w_expert_up, w_expert_down, ln2_g, ln2_b)
        n_p = h_p.shape[0] * h_p.shape[1]
        h_p, k_p, v_p, c_p = _encoder_layer(
            h_p, None, None, None, lw, alpha=alpha,
            proj_tm=min(512, n_p), post_tm=min(512, h_p.shape[1]), moe_tm=min(1024, n_p),
            tq=128, pairs=4)
        ts = h_s.shape[1]
        h_s, k_s, v_s, c_s = _encoder_layer(
            h_s, cache_k[l], cache_v[l], state_conv[l], lw, alpha=alpha,
            proj_tm=h_s.shape[0] * ts, post_tm=ts, moe_tm=h_s.shape[0] * ts, tq=ts, pairs=1)
        for lst, val in zip(outs, (k_p, v_p, c_p, k_s, v_s, c_s)):
            lst.append(val)
    return (h_p, h_s) + tuple(jnp.stack(o) for o in outs)
```

```python
import functools

import jax
import jax.numpy as jnp
from jax import lax
from jax.experimental import pallas as pl
from jax.experimental.pallas import tpu as pltpu

F32 = jnp.float32
BF16 = jnp.bfloat16

D_MODEL = 1024
SB_DIM = 512
SB_HEADS = 8
SB_HEAD_DIM = 64
SB_SCALE = SB_HEAD_DIM ** -0.5
CONV_DIM = 512
CONV_K = 31
N_GROUPS = 4
EXPERTS_PER_GROUP = 4
N_EXPERTS = 16
EXPERT_DFF = 256
LN_EPS = 1e-5
IN_DIM = 3 * SB_DIM + 2 * CONV_DIM

LANES = 128
SUBLANES = 8
CONV_CHUNK = 32
MXU_COLS = 256
KEY_TILE = 128
HALO = 32
ROUTER_LANES = 128
VMEM_LIMIT = 56 * 1024 * 1024
EXP_UNDERFLOW = -88.0
MASKED_LOG_WEIGHT = -1e30


def _layer_norm(x, g, b):
    mu = jnp.mean(x, axis=-1, keepdims=True)
    xc = x - mu
    var = jnp.mean(xc * xc, axis=-1, keepdims=True)
    return xc * lax.rsqrt(var + LN_EPS) * g + b


def _proj_kernel(x_ref, w_ref, q_ref, k_ref, v_ref, kb_ref, vb_ref, u_ref):
    xb = x_ref[...].astype(BF16)

    def mm(lo, hi):
        return jnp.dot(xb, w_ref[:, lo:hi], preferred_element_type=F32)

    q_ref[...] = (mm(0, SB_DIM) * SB_SCALE).astype(BF16)
    k = mm(SB_DIM, 2 * SB_DIM)
    k_ref[...] = k
    kb_ref[...] = k.astype(BF16)
    v = mm(2 * SB_DIM, 3 * SB_DIM)
    v_ref[...] = v
    vb_ref[...] = v.astype(BF16)
    val = mm(3 * SB_DIM, 3 * SB_DIM + CONV_DIM)
    gate = mm(3 * SB_DIM + CONV_DIM, IN_DIM)
    u_ref[...] = val * jax.nn.sigmoid(gate)


def _proj(x2d, w_in_bf, tm):
    n = x2d.shape[0]
    row = lambda i: (i, 0)
    outs = [jax.ShapeDtypeStruct((n, SB_DIM), BF16),
            jax.ShapeDtypeStruct((n, SB_DIM), F32),
            jax.ShapeDtypeStruct((n, SB_DIM), F32),
            jax.ShapeDtypeStruct((n, SB_DIM), BF16),
            jax.ShapeDtypeStruct((n, SB_DIM), BF16),
            jax.ShapeDtypeStruct((n, CONV_DIM), F32)]
    return pl.pallas_call(
        _proj_kernel,
        grid=(n // tm,),
        in_specs=[pl.BlockSpec((tm, D_MODEL), row),
                  pl.BlockSpec((D_MODEL, IN_DIM), lambda i: (0, 0))],
        out_specs=[pl.BlockSpec((tm, SB_DIM), row)] * 5 + [pl.BlockSpec((tm, CONV_DIM), row)],
        out_shape=outs,
        compiler_params=pltpu.CompilerParams(dimension_semantics=("arbitrary",),
                                             vmem_limit_bytes=VMEM_LIMIT),
        name="proj",
    )(x2d, w_in_bf)


def _attn_kernel(*refs, tq, pairs, n_cache_tiles):
    if n_cache_tiles:
        (q_ref, kn_ref, vn_ref, kc_ref, vc_ref, uu_ref, o_ref,
         qs_ref, acc_ref, run_ref, zs_ref, vs_ref) = refs
    else:
        q_ref, kn_ref, vn_ref, uu_ref, o_ref, qs_ref, acc_ref, run_ref, zs_ref, vs_ref = refs
    i = pl.program_id(2)
    rows = 2 * tq
    uu = uu_ref[...]

    lane_q = lax.broadcasted_iota(jnp.int32, (tq, LANES), 1)
    lane_k = lax.broadcasted_iota(jnp.int32, (KEY_TILE, LANES), 1)
    r_idx = lax.broadcasted_iota(jnp.int32, (rows, KEY_TILE), 0)
    r_idx = jnp.where(r_idx >= tq, r_idx - tq, r_idx)
    c_idx = lax.broadcasted_iota(jnp.int32, (rows, KEY_TILE), 1)
    causal = c_idx < r_idx

    def score_sweep(slot, load_kv, mask, first):
        zs, splits = [], []
        for p in range(pairs):
            kt, vt = load_kv(slice(p * LANES, (p + 1) * LANES))
            vs_ref[slot, p, 0] = jnp.where(lane_k < SB_HEAD_DIM, vt, jnp.zeros_like(vt))
            vs_ref[slot, p, 1] = jnp.where(lane_k >= SB_HEAD_DIM, vt, jnp.zeros_like(vt))
            z = lax.dot_general(qs_ref[p], kt, (((1,), (1,)), ((), ())),
                                preferred_element_type=F32)
            lk = -(jnp.maximum(z, 0.0) + jnp.log(1.0 + jnp.exp(-jnp.abs(z))))
            if mask is not None:
                lk = jnp.where(mask, lk, 0.0)
            hi = lk.astype(BF16)
            lo = (lk - hi.astype(F32)).astype(BF16)
            zs.append(z)
            splits.append(jnp.concatenate([hi, lo], axis=1))
        su = jnp.dot(jnp.concatenate(splits, axis=0), uu, preferred_element_type=F32)
        worst = None
        for p in range(pairs):
            sp = su[p * rows:(p + 1) * rows]
            logw = zs[p] + sp[:, :KEY_TILE]
            tot = sp[:, KEY_TILE:]
            if not first:
                run = run_ref[p]
                logw = logw + run
                tot = tot + run
            if mask is not None:
                logw = jnp.where(mask, logw, MASKED_LOG_WEIGHT)
            zs_ref[slot, p] = logw
            run_ref[p] = tot
            worst = tot if worst is None else jnp.maximum(worst, tot)
        return jnp.max(worst)

    def value_sweep(slot):
        for p in range(pairs):
            wb = jnp.exp(zs_ref[slot, p]).astype(BF16)
            acc_ref[p] += (jnp.dot(wb[:tq], vs_ref[slot, p, 0], preferred_element_type=F32)
                           + jnp.dot(wb[tq:], vs_ref[slot, p, 1], preferred_element_type=F32))

    for p in range(pairs):
        qp = q_ref[:, p * LANES:(p + 1) * LANES]
        qs_ref[p] = jnp.concatenate(
            [jnp.where(lane_q < SB_HEAD_DIM, qp, jnp.zeros_like(qp)),
             jnp.where(lane_q >= SB_HEAD_DIM, qp, jnp.zeros_like(qp))], axis=0)

    def new_tile(j):
        off = pl.multiple_of(j * KEY_TILE, KEY_TILE)
        return lambda cols: (kn_ref[pl.ds(off, KEY_TILE), cols], vn_ref[pl.ds(off, KEY_TILE), cols])

    def cache_tile(j):
        off = pl.multiple_of(j * KEY_TILE, KEY_TILE)
        return lambda cols: (kc_ref[pl.ds(off, KEY_TILE), cols].astype(BF16),
                             vc_ref[pl.ds(off, KEY_TILE), cols].astype(BF16))

    live = lambda c: (c[0] >= 0) & (c[2] > EXP_UNDERFLOW)

    def pipelined(tile_of):
        def body(c):
            j, slot, _ = c
            value_sweep(slot)
            return j - 1, 1 - slot, score_sweep(1 - slot, tile_of(j), None, False)
        return body

    worst = score_sweep(0, new_tile(i), causal, True)
    acc_ref[...] = jnp.zeros_like(acc_ref)
    carry = lax.while_loop(live, pipelined(new_tile), (i - 1, jnp.int32(0), worst))
    if n_cache_tiles:
        carry = lax.while_loop(live, pipelined(cache_tile),
                               (jnp.int32(n_cache_tiles - 1), carry[1], carry[2]))
    value_sweep(carry[1])

    for p in range(pairs):
        o_ref[:, p * LANES:(p + 1) * LANES] = acc_ref[p].astype(BF16)


def _suffix_sum_matrix():
    r = lax.broadcasted_iota(jnp.int32, (2 * KEY_TILE, 2 * KEY_TILE), 0) % KEY_TILE
    c = lax.broadcasted_iota(jnp.int32, (2 * KEY_TILE, 2 * KEY_TILE), 1)
    return jnp.where((c >= KEY_TILE) | (r >= c), 1.0, 0.0).astype(BF16)


def _attention(q, kn, vn, cache_k, cache_v, *, bsz, tq, nq, new_rows, pairs):
    n = q.shape[0]
    width = pairs * LANES
    steps = SB_DIM // width
    n_cache_tiles = 0 if cache_k is None else cache_k.shape[0] // bsz // KEY_TILE
    in_specs = [pl.BlockSpec((tq, width), lambda b, p, i: (b * nq + i, p)),
                pl.BlockSpec((new_rows, width), lambda b, p, i: (b, p)),
                pl.BlockSpec((new_rows, width), lambda b, p, i: (b, p))]
    args = [q, kn, vn]
    if n_cache_tiles:
        past = n_cache_tiles * KEY_TILE
        in_specs += [pl.BlockSpec((past, width), lambda b, p, i: (b, p))] * 2
        args += [cache_k, cache_v]
    in_specs.append(pl.BlockSpec((2 * KEY_TILE, 2 * KEY_TILE), lambda b, p, i: (0, 0)))
    args.append(_suffix_sum_matrix())
    return pl.pallas_call(
        functools.partial(_attn_kernel, tq=tq, pairs=pairs, n_cache_tiles=n_cache_tiles),
        grid=(bsz, steps, nq),
        in_specs=in_specs,
        out_specs=pl.BlockSpec((tq, width), lambda b, p, i: (b * nq + i, p)),
        out_shape=jax.ShapeDtypeStruct((n, SB_DIM), BF16),
        scratch_shapes=[pltpu.VMEM((pairs, 2 * tq, LANES), BF16),
                        pltpu.VMEM((pairs, tq, LANES), F32),
                        pltpu.VMEM((pairs, 2 * tq, KEY_TILE), F32),
                        pltpu.VMEM((2, pairs, 2 * tq, KEY_TILE), F32),
                        pltpu.VMEM((2, pairs, 2, KEY_TILE, LANES), BF16)],
        compiler_params=pltpu.CompilerParams(
            dimension_semantics=("arbitrary", "arbitrary", "arbitrary"),
            vmem_limit_bytes=VMEM_LIMIT),
        name="attn",
    )(*args)


def _post_kernel(x_ref, attn_ref, u_ref, halo_ref, hist_ref, wba_ref, wbc_ref, cw_ref, cb_ref,
                 clg_ref, clb_ref, wg_ref, bg_ref, wo_ref, l1g_ref, l1b_ref, wr_ref, br_ref,
                 h_ref, comb_ref, ext_ref, sh_ref, conv_ref, xb_ref, gates_ref, ba_ref, *, tm, alpha):
    i = pl.program_id(1)
    prev = jnp.where(i == 0, hist_ref[...], halo_ref[...])
    ext_ref[0:HALO, :] = prev
    ext_ref[HALO:HALO + tm, :] = u_ref[...]
    span = tm + HALO - SUBLANES
    for s in range(1, SUBLANES):
        sh_ref[s - 1] = ext_ref[s:s + span, :]
    xb_ref[...] = x_ref[...].astype(BF16)

    def gate_block(n):
        cols = slice(n * MXU_COLS, (n + 1) * MXU_COLS)
        gates_ref[:, cols] = jax.nn.sigmoid(
            jnp.dot(xb_ref[...], wg_ref[:, cols], preferred_element_type=F32) + bg_ref[:, cols])

    def attn_block(n):
        cols = slice(n * MXU_COLS, (n + 1) * MXU_COLS)
        ba_ref[:, cols] = jnp.dot(attn_ref[...], wba_ref[:, cols], preferred_element_type=F32)

    jobs = ([functools.partial(gate_block, n) for n in range(2 * D_MODEL // MXU_COLS)]
            + [functools.partial(attn_block, n) for n in range(D_MODEL // MXU_COLS)])

    first = HALO - (CONV_K - 1)
    chunk = min(CONV_CHUNK, tm)
    tiles = chunk // SUBLANES
    for idx, r0 in enumerate(range(0, tm, chunk)):
        acc = jnp.zeros((tiles, SUBLANES, CONV_DIM), F32) + cb_ref[...]
        for j in range(CONV_K):
            phase = (first + j) % SUBLANES
            base = first + j - phase + r0
            rows = ext_ref[base:base + chunk, :] if phase == 0 else sh_ref[phase - 1, base:base + chunk, :]
            acc = acc + rows.reshape(tiles, SUBLANES, CONV_DIM) * cw_ref[j]
        conv_ref[r0:r0 + chunk, :] = acc.reshape(chunk, CONV_DIM)
        if jobs:
            jobs.pop(0)()
    while jobs:
        jobs.pop(0)()

    c = _layer_norm(conv_ref[...], clg_ref[...], clb_ref[...])
    c = c * jax.nn.sigmoid(c)
    branch_b = jnp.dot(c.astype(BF16), wbc_ref[...], preferred_element_type=F32)
    x = x_ref[...]
    m = gates_ref[:, :D_MODEL] * ba_ref[...] + gates_ref[:, D_MODEL:] * branch_b
    mixed = jnp.dot(m.astype(BF16), wo_ref[...], preferred_element_type=F32)
    h = _layer_norm(alpha * x + mixed, l1g_ref[...], l1b_ref[...])
    h_ref[...] = h

    logits = jnp.dot(h.astype(BF16), wr_ref[...], preferred_element_type=F32) + br_ref[...]
    lane = lax.broadcasted_iota(jnp.int32, (tm, ROUTER_LANES), 1)
    neg = jnp.float32(-jnp.inf)
    gl = jnp.where(lane < N_GROUPS, logits, neg)
    gmax = jnp.max(gl, axis=-1, keepdims=True)
    g_idx = jnp.min(jnp.where(gl == gmax, lane, ROUTER_LANES), axis=-1, keepdims=True)
    p_group = 1.0 / jnp.sum(jnp.exp(gl - gmax), axis=-1, keepdims=True)
    lo = N_GROUPS + EXPERTS_PER_GROUP * g_idx
    el = jnp.where((lane >= lo) & (lane < lo + EXPERTS_PER_GROUP), logits, neg)
    top1 = jnp.max(el, axis=-1, keepdims=True)
    i1 = jnp.min(jnp.where(el == top1, lane, ROUTER_LANES), axis=-1, keepdims=True)
    el2 = jnp.where(lane == i1, neg, el)
    top2 = jnp.max(el2, axis=-1, keepdims=True)
    i2 = jnp.min(jnp.where(el2 == top2, lane, ROUTER_LANES), axis=-1, keepdims=True)
    e2 = jnp.exp(top2 - top1)
    p1 = 1.0 / (1.0 + e2)
    p2 = e2 / (1.0 + e2)
    comb_ref[...] = (jnp.where(lane == i1, p_group * p1, 0.0)
                     + jnp.where(lane == i2, p_group * p2, 0.0))


def _post(x2d, attn, u, hist, lw, *, bsz, tlen, tm, alpha):
    n = x2d.shape[0]
    nt = tlen // tm
    row = lambda b, i: (b * nt + i, 0)
    const = lambda b, i: (0, 0)

    def halo_map(b, i):
        return (jnp.maximum((b * tlen + i * tm) // HALO - 1, 0), 0)

    full = lambda a: pl.BlockSpec(a.shape, lambda b, i, nd=a.ndim: (0,) * nd)
    weights = [lw["wba"], lw["wbc"], lw["conv_w"], lw["conv_b"], lw["conv_ln_g"], lw["conv_ln_b"],
               lw["w_gate"], lw["b_gate"], lw["w_out"], lw["ln1_g"], lw["ln1_b"],
               lw["w_router"], lw["b_router"]]
    return pl.pallas_call(
        functools.partial(_post_kernel, tm=tm, alpha=alpha),
        grid=(bsz, nt),
        in_specs=[pl.BlockSpec((tm, D_MODEL), row),
                  pl.BlockSpec((tm, SB_DIM), row),
                  pl.BlockSpec((tm, CONV_DIM), row),
                  pl.BlockSpec((HALO, CONV_DIM), halo_map),
                  pl.BlockSpec((HALO, CONV_DIM), lambda b, i: (b, 0))] + [full(w) for w in weights],
        out_specs=[pl.BlockSpec((tm, D_MODEL), row), pl.BlockSpec((tm, ROUTER_LANES), row)],
        out_shape=[jax.ShapeDtypeStruct((n, D_MODEL), F32),
                   jax.ShapeDtypeStruct((n, ROUTER_LANES), F32)],
        scratch_shapes=[pltpu.VMEM((HALO + tm, CONV_DIM), F32),
                        pltpu.VMEM((SUBLANES - 1, HALO + tm - SUBLANES, CONV_DIM), F32),
                        pltpu.VMEM((tm, CONV_DIM), F32),
                        pltpu.VMEM((tm, D_MODEL), BF16),
                        pltpu.VMEM((tm, 2 * D_MODEL), F32),
                        pltpu.VMEM((tm, D_MODEL), F32)],
        compiler_params=pltpu.CompilerParams(dimension_semantics=("arbitrary", "arbitrary"),
                                             vmem_limit_bytes=VMEM_LIMIT),
        name="post",
    )(x2d, attn, u, u, hist, *weights)


def _moe_kernel(h_ref, comb_ref, wg_ref, wu_ref, wd_ref, l2g_ref, l2b_ref, y_ref,
                hb_ref, acc_ref, *, tm, alpha):
    e = pl.program_id(1)

    @pl.when(e == 0)
    def _():
        hb_ref[...] = h_ref[...].astype(BF16)
        acc_ref[...] = jnp.zeros_like(acc_ref)

    hb = hb_ref[...]
    gate = jnp.dot(hb, wg_ref[0], preferred_element_type=F32)
    up = jnp.dot(hb, wu_ref[0], preferred_element_type=F32)
    lane = lax.broadcasted_iota(jnp.int32, (tm, ROUTER_LANES), 1)
    c_e = jnp.sum(jnp.where(lane == N_GROUPS + e, comb_ref[...], 0.0), axis=-1, keepdims=True)
    hid = gate * jax.nn.sigmoid(gate) * up * c_e
    acc_ref[...] += jnp.dot(hid.astype(BF16), wd_ref[0], preferred_element_type=F32)

    @pl.when(e == N_EXPERTS - 1)
    def _():
        y_ref[...] = _layer_norm(alpha * h_ref[...] + acc_ref[...], l2g_ref[...], l2b_ref[...])


def _moe(h, comb, lw, *, tm, alpha):
    n = h.shape[0]
    row = lambda i, e: (i, 0)
    const = lambda i, e: (0, 0)
    return pl.pallas_call(
        functools.partial(_moe_kernel, tm=tm, alpha=alpha),
        grid=(n // tm, N_EXPERTS),
        in_specs=[pl.BlockSpec((tm, D_MODEL), row),
                  pl.BlockSpec((tm, ROUTER_LANES), row),
                  pl.BlockSpec((1, D_MODEL, EXPERT_DFF), lambda i, e: (e, 0, 0)),
                  pl.BlockSpec((1, D_MODEL, EXPERT_DFF), lambda i, e: (e, 0, 0)),
                  pl.BlockSpec((1, EXPERT_DFF, D_MODEL), lambda i, e: (e, 0, 0)),
                  pl.BlockSpec((1, D_MODEL), const),
                  pl.BlockSpec((1, D_MODEL), const)],
        out_specs=pl.BlockSpec((tm, D_MODEL), row),
        out_shape=jax.ShapeDtypeStruct((n, D_MODEL), F32),
        scratch_shapes=[pltpu.VMEM((tm, D_MODEL), BF16), pltpu.VMEM((tm, D_MODEL), F32)],
        compiler_params=pltpu.CompilerParams(dimension_semantics=("arbitrary", "arbitrary"),
                                             vmem_limit_bytes=VMEM_LIMIT),
        name="moe",
    )(h, comb, lw["w_expert_gate"], lw["w_expert_up"], lw["w_expert_down"],
      lw["ln2_g"], lw["ln2_b"])


def _encoder_layer(x, k_hist, v_hist, conv_hist, lw, *, alpha, proj_tm, post_tm, moe_tm, tq, pairs):
    bsz, tlen, _ = x.shape
    n = bsz * tlen
    x2d = x.reshape(n, D_MODEL)
    q, k, v, kb, vb, u = _proj(x2d, lw["w_in"], proj_tm)

    nq = tlen // tq
    if k_hist is None:
        kn, vn, new_rows, ck, cv = kb, vb, tlen, None, None
    else:
        pad = lambda a: jnp.pad(a.reshape(bsz, tlen, SB_DIM),
                                ((0, 0), (0, KEY_TILE - tlen), (0, 0))).reshape(bsz * KEY_TILE, SB_DIM)
        kn, vn, new_rows = pad(kb), pad(vb), KEY_TILE
        past = k_hist.shape[1]
        ck = k_hist.reshape(bsz * past, SB_DIM)
        cv = v_hist.reshape(bsz * past, SB_DIM)
    attn = _attention(q, kn, vn, ck, cv, bsz=bsz, tq=tq, nq=nq, new_rows=new_rows, pairs=pairs)

    if conv_hist is None:
        hist = jnp.zeros((bsz * HALO, CONV_DIM), F32)
    else:
        hist = jnp.pad(conv_hist, ((0, 0), (HALO - (CONV_K - 1), 0), (0, 0))).reshape(bsz * HALO, CONV_DIM)
    h, comb = _post(x2d, attn, u, hist, lw, bsz=bsz, tlen=tlen, tm=post_tm, alpha=alpha)
    y = _moe(h, comb, lw, tm=moe_tm, alpha=alpha)

    u3 = u.reshape(bsz, tlen, CONV_DIM)
    if conv_hist is None:
        conv_state = u3[:, tlen - (CONV_K - 1):]
    else:
        conv_state = jnp.concatenate([conv_hist, u3], axis=1)[:, -(CONV_K - 1):]
    return (y.reshape(bsz, tlen, D_MODEL),
            k.reshape(bsz, tlen, SB_HEADS, SB_HEAD_DIM),
            v.reshape(bsz, tlen, SB_HEADS, SB_HEAD_DIM),
            conv_state)


def _layer_weights(l, w_in, w_branch_attn, w_branch_conv, conv_w, conv_b, conv_ln_g, conv_ln_b,
                   w_gate, b_gate, w_out, ln1_g, ln1_b, w_router_group, b_router_group,
                   w_router_expert, b_router_expert, w_expert_gate, w_expert_up, w_expert_down,
                   ln2_g, ln2_b):
    n_route = N_GROUPS + N_EXPERTS
    w_router = jnp.concatenate([w_router_group[l], w_router_expert[l]], axis=1)
    w_router = jnp.pad(w_router, ((0, 0), (0, ROUTER_LANES - n_route)))
    b_router = jnp.pad(jnp.concatenate([b_router_group[l], b_router_expert[l]]),
                       (0, ROUTER_LANES - n_route))
    row = lambda a: a.reshape(1, -1).astype(F32)
    return dict(
        w_in=w_in[l].astype(BF16), wba=w_branch_attn[l].astype(BF16),
        wbc=w_branch_conv[l].astype(BF16),
        conv_w=jnp.broadcast_to(conv_w[l].astype(F32)[:, None, :], (CONV_K, SUBLANES, CONV_DIM)),
        conv_b=row(conv_b[l]),
        conv_ln_g=row(conv_ln_g[l]), conv_ln_b=row(conv_ln_b[l]),
        w_gate=w_gate[l].astype(BF16), b_gate=row(b_gate[l]), w_out=w_out[l].astype(BF16),
        ln1_g=row(ln1_g[l]), ln1_b=row(ln1_b[l]),
        w_router=w_router.astype(BF16), b_router=row(b_router),
        w_expert_gate=w_expert_gate[l].astype(BF16), w_expert_up=w_expert_up[l].astype(BF16),
        w_expert_down=w_expert_down[l].astype(BF16), ln2_g=row(ln2_g[l]), ln2_b=row(ln2_b[l]))


def kernel(x_prompt, x_sample, cache_k, cache_v, state_conv, w_in, w_branch_attn, w_branch_conv, conv_w, conv_b, conv_ln_g, conv_ln_b, w_gate, b_gate, w_out, ln1_g, ln1_b, w_router_group, b_router_group, w_router_expert, b_router_expert, w_expert_gate, w_expert_up, w_expert_down, ln2_g, ln2_b):
    depth = w_in.shape[0]
    alpha = (2.0 * depth) ** 0.25
    h_p, h_s = x_prompt, x_sample
    outs = [[] for _ in range(6)]
    for l in range(depth):
        lw = _layer_weights(l, w_in, w_branch_attn, w_branch_conv, conv_w, conv_b, conv_ln_g,
                            conv_ln_b, w_gate, b_gate, w_out, ln1_g, ln1_b, w_router_group,
                            b_router_group, w_router_expert, b_router_expert, w_expert_gate,
                            w_expert_up, w_expert_down, ln2_g, ln2_b)
        n_p = h_p.shape[0] * h_p.shape[1]
        h_p, k_p, v_p, c_p = _encoder_layer(
            h_p, None, None, None, lw, alpha=alpha,
            proj_tm=min(512, n_p), post_tm=min(512, h_p.shape[1]), moe_tm=min(1024, n_p),
            tq=128, pairs=4)
        ts = h_s.shape[1]
        h_s, k_s, v_s, c_s = _encoder_layer(
            h_s, cache_k[l], cache_v[l], state_conv[l], lw, alpha=alpha,
            proj_tm=h_s.shape[0] * ts, post_tm=ts, moe_tm=h_s.shape[0] * ts, tq=ts, pairs=1)
        for lst, val in zip(outs, (k_p, v_p, c_p, k_s, v_s, c_s)):
            lst.append(val)
    return (h_p, h_s) + tuple(jnp.stack(o) for o in outs)
```

```python
import functools

import jax
import jax.numpy as jnp
from jax import lax
from jax.experimental import pallas as pl
from jax.experimental.pallas import tpu as pltpu

F32 = jnp.float32
BF16 = jnp.bfloat16

D_MODEL = 1024
SB_DIM = 512
SB_HEADS = 8
SB_HEAD_DIM = 64
SB_SCALE = SB_HEAD_DIM ** -0.5
CONV_DIM = 512
CONV_K = 31
N_GROUPS = 4
EXPERTS_PER_GROUP = 4
N_EXPERTS = 16
EXPERT_DFF = 256
LN_EPS = 1e-5
IN_DIM = 3 * SB_DIM + 2 * CONV_DIM

LANES = 128
SUBLANES = 8
CONV_CHUNK = 32
MXU_COLS = 256
KEY_TILE = 128
HALO = 32
ROUTER_LANES = 128
ROW_WIDTH = D_MODEL + ROUTER_LANES
MOE_TILE = 256
MOVE_TILE = 512
VMEM_LIMIT = 56 * 1024 * 1024
EXP_UNDERFLOW = -88.0
MASKED_LOG_WEIGHT = -1e30


def _layer_norm(x, g, b):
    mu = jnp.mean(x, axis=-1, keepdims=True)
    xc = x - mu
    var = jnp.mean(xc * xc, axis=-1, keepdims=True)
    return xc * lax.rsqrt(var + LN_EPS) * g + b


def _proj_kernel(x_ref, w_ref, q_ref, k_ref, v_ref, kb_ref, vb_ref, u_ref):
    xb = x_ref[...].astype(BF16)

    def mm(lo, hi):
        return jnp.dot(xb, w_ref[:, lo:hi], preferred_element_type=F32)

    q_ref[...] = (mm(0, SB_DIM) * SB_SCALE).astype(BF16)
    k = mm(SB_DIM, 2 * SB_DIM)
    k_ref[...] = k
    kb_ref[...] = k.astype(BF16)
    v = mm(2 * SB_DIM, 3 * SB_DIM)
    v_ref[...] = v
    vb_ref[...] = v.astype(BF16)
    val = mm(3 * SB_DIM, 3 * SB_DIM + CONV_DIM)
    gate = mm(3 * SB_DIM + CONV_DIM, IN_DIM)
    u_ref[...] = val * jax.nn.sigmoid(gate)


def _proj(x2d, w_in_bf, tm):
    n = x2d.shape[0]
    row = lambda i: (i, 0)
    outs = [jax.ShapeDtypeStruct((n, SB_DIM), BF16),
            jax.ShapeDtypeStruct((n, SB_DIM), F32),
            jax.ShapeDtypeStruct((n, SB_DIM), F32),
            jax.ShapeDtypeStruct((n, SB_DIM), BF16),
            jax.ShapeDtypeStruct((n, SB_DIM), BF16),
            jax.ShapeDtypeStruct((n, CONV_DIM), F32)]
    return pl.pallas_call(
        _proj_kernel,
        grid=(n // tm,),
        in_specs=[pl.BlockSpec((tm, D_MODEL), row),
                  pl.BlockSpec((D_MODEL, IN_DIM), lambda i: (0, 0))],
        out_specs=[pl.BlockSpec((tm, SB_DIM), row)] * 5 + [pl.BlockSpec((tm, CONV_DIM), row)],
        out_shape=outs,
        compiler_params=pltpu.CompilerParams(dimension_semantics=("arbitrary",),
                                             vmem_limit_bytes=VMEM_LIMIT),
        name="proj",
    )(x2d, w_in_bf)


def _attn_kernel(*refs, tq, pairs, n_cache_tiles):
    if n_cache_tiles:
        (q_ref, kn_ref, vn_ref, kc_ref, vc_ref, uu_ref, o_ref,
         qs_ref, acc_ref, run_ref, zs_ref, vs_ref) = refs
    else:
        q_ref, kn_ref, vn_ref, uu_ref, o_ref, qs_ref, acc_ref, run_ref, zs_ref, vs_ref = refs
    i = pl.program_id(2)
    rows = 2 * tq
    uu = uu_ref[...]

    lane_q = lax.broadcasted_iota(jnp.int32, (tq, LANES), 1)
    lane_k = lax.broadcasted_iota(jnp.int32, (KEY_TILE, LANES), 1)
    r_idx = lax.broadcasted_iota(jnp.int32, (rows, KEY_TILE), 0)
    r_idx = jnp.where(r_idx >= tq, r_idx - tq, r_idx)
    c_idx = lax.broadcasted_iota(jnp.int32, (rows, KEY_TILE), 1)
    causal = c_idx < r_idx

    def score_sweep(slot, load_kv, mask, first):
        zs, splits = [], []
        for p in range(pairs):
            kt, vt = load_kv(slice(p * LANES, (p + 1) * LANES))
            vs_ref[slot, p, 0] = jnp.where(lane_k < SB_HEAD_DIM, vt, jnp.zeros_like(vt))
            vs_ref[slot, p, 1] = jnp.where(lane_k >= SB_HEAD_DIM, vt, jnp.zeros_like(vt))
            z = lax.dot_general(qs_ref[p], kt, (((1,), (1,)), ((), ())),
                                preferred_element_type=F32)
            lk = -(jnp.maximum(z, 0.0) + jnp.log(1.0 + jnp.exp(-jnp.abs(z))))
            if mask is not None:
                lk = jnp.where(mask, lk, 0.0)
            hi = lk.astype(BF16)
            lo = (lk - hi.astype(F32)).astype(BF16)
            zs.append(z)
            splits.append(jnp.concatenate([hi, lo], axis=1))
        su = jnp.dot(jnp.concatenate(splits, axis=0), uu, preferred_element_type=F32)
        worst = None
        for p in range(pairs):
            sp = su[p * rows:(p + 1) * rows]
            logw = zs[p] + sp[:, :KEY_TILE]
            tot = sp[:, KEY_TILE:]
            if not first:
                run = run_ref[p]
                logw = logw + run
                tot = tot + run
            if mask is not None:
                logw = jnp.where(mask, logw, MASKED_LOG_WEIGHT)
            zs_ref[slot, p] = logw
            run_ref[p] = tot
            worst = tot if worst is None else jnp.maximum(worst, tot)
        return jnp.max(worst)

    def value_sweep(slot):
        for p in range(pairs):
            wb = jnp.exp(zs_ref[slot, p]).astype(BF16)
            acc_ref[p] += (jnp.dot(wb[:tq], vs_ref[slot, p, 0], preferred_element_type=F32)
                           + jnp.dot(wb[tq:], vs_ref[slot, p, 1], preferred_element_type=F32))

    for p in range(pairs):
        qp = q_ref[:, p * LANES:(p + 1) * LANES]
        qs_ref[p] = jnp.concatenate(
            [jnp.where(lane_q < SB_HEAD_DIM, qp, jnp.zeros_like(qp)),
             jnp.where(lane_q >= SB_HEAD_DIM, qp, jnp.zeros_like(qp))], axis=0)

    def new_tile(j):
        off = pl.multiple_of(j * KEY_TILE, KEY_TILE)
        return lambda cols: (kn_ref[pl.ds(off, KEY_TILE), cols], vn_ref[pl.ds(off, KEY_TILE), cols])

    def cache_tile(j):
        off = pl.multiple_of(j * KEY_TILE, KEY_TILE)
        return lambda cols: (kc_ref[pl.ds(off, KEY_TILE), cols].astype(BF16),
                             vc_ref[pl.ds(off, KEY_TILE), cols].astype(BF16))

    live = lambda c: (c[0] >= 0) & (c[2] > EXP_UNDERFLOW)

    def pipelined(tile_of):
        def body(c):
            j, slot, _ = c
            value_sweep(slot)
            return j - 1, 1 - slot, score_sweep(1 - slot, tile_of(j), None, False)
        return body

    worst = score_sweep(0, new_tile(i), causal, True)
    acc_ref[...] = jnp.zeros_like(acc_ref)
    carry = lax.while_loop(live, pipelined(new_tile), (i - 1, jnp.int32(0), worst))
    if n_cache_tiles:
        carry = lax.while_loop(live, pipelined(cache_tile),
                               (jnp.int32(n_cache_tiles - 1), carry[1], carry[2]))
    value_sweep(carry[1])

    for p in range(pairs):
        o_ref[:, p * LANES:(p + 1) * LANES] = acc_ref[p].astype(BF16)


def _suffix_sum_matrix():
    r = lax.broadcasted_iota(jnp.int32, (2 * KEY_TILE, 2 * KEY_TILE), 0) % KEY_TILE
    c = lax.broadcasted_iota(jnp.int32, (2 * KEY_TILE, 2 * KEY_TILE), 1)
    return jnp.where((c >= KEY_TILE) | (r >= c), 1.0, 0.0).astype(BF16)


def _attention(q, kn, vn, cache_k, cache_v, *, bsz, tq, nq, new_rows, pairs):
    n = q.shape[0]
    width = pairs * LANES
    steps = SB_DIM // width
    n_cache_tiles = 0 if cache_k is None else cache_k.shape[0] // bsz // KEY_TILE
    in_specs = [pl.BlockSpec((tq, width), lambda b, p, i: (b * nq + i, p)),
                pl.BlockSpec((new_rows, width), lambda b, p, i: (b, p)),
                pl.BlockSpec((new_rows, width), lambda b, p, i: (b, p))]
    args = [q, kn, vn]
    if n_cache_tiles:
        past = n_cache_tiles * KEY_TILE
        in_specs += [pl.BlockSpec((past, width), lambda b, p, i: (b, p))] * 2
        args += [cache_k, cache_v]
    in_specs.append(pl.BlockSpec((2 * KEY_TILE, 2 * KEY_TILE), lambda b, p, i: (0, 0)))
    args.append(_suffix_sum_matrix())
    return pl.pallas_call(
        functools.partial(_attn_kernel, tq=tq, pairs=pairs, n_cache_tiles=n_cache_tiles),
        grid=(bsz, steps, nq),
        in_specs=in_specs,
        out_specs=pl.BlockSpec((tq, width), lambda b, p, i: (b * nq + i, p)),
        out_shape=jax.ShapeDtypeStruct((n, SB_DIM), BF16),
        scratch_shapes=[pltpu.VMEM((pairs, 2 * tq, LANES), BF16),
                        pltpu.VMEM((pairs, tq, LANES), F32),
                        pltpu.VMEM((pairs, 2 * tq, KEY_TILE), F32),
                        pltpu.VMEM((2, pairs, 2 * tq, KEY_TILE), F32),
                        pltpu.VMEM((2, pairs, 2, KEY_TILE, LANES), BF16)],
        compiler_params=pltpu.CompilerParams(
            dimension_semantics=("arbitrary", "arbitrary", "arbitrary"),
            vmem_limit_bytes=VMEM_LIMIT),
        name="attn",
    )(*args)


def _post_kernel(x_ref, attn_ref, u_ref, halo_ref, hist_ref, wba_ref, wbc_ref, cw_ref, cb_ref,
                 clg_ref, clb_ref, wg_ref, bg_ref, wo_ref, l1g_ref, l1b_ref, wr_ref, br_ref,
                 lt_ref, hc_ref, cnt_ref, ext_ref, sh_ref, conv_ref, xb_ref, gates_ref, ba_ref,
                 *, tm, alpha):
    i = pl.program_id(1)
    prev = jnp.where(i == 0, hist_ref[...], halo_ref[...])
    ext_ref[0:HALO, :] = prev
    ext_ref[HALO:HALO + tm, :] = u_ref[...]
    span = tm + HALO - SUBLANES
    for s in range(1, SUBLANES):
        sh_ref[s - 1] = ext_ref[s:s + span, :]
    xb_ref[...] = x_ref[...].astype(BF16)

    def gate_block(n):
        cols = slice(n * MXU_COLS, (n + 1) * MXU_COLS)
        gates_ref[:, cols] = jax.nn.sigmoid(
            jnp.dot(xb_ref[...], wg_ref[:, cols], preferred_element_type=F32) + bg_ref[:, cols])

    def attn_block(n):
        cols = slice(n * MXU_COLS, (n + 1) * MXU_COLS)
        ba_ref[:, cols] = jnp.dot(attn_ref[...], wba_ref[:, cols], preferred_element_type=F32)

    jobs = ([functools.partial(gate_block, n) for n in range(2 * D_MODEL // MXU_COLS)]
            + [functools.partial(attn_block, n) for n in range(D_MODEL // MXU_COLS)])

    first = HALO - (CONV_K - 1)
    chunk = min(CONV_CHUNK, tm)
    tiles = chunk // SUBLANES
    for idx, r0 in enumerate(range(0, tm, chunk)):
        acc = jnp.zeros((tiles, SUBLANES, CONV_DIM), F32) + cb_ref[...]
        for j in range(CONV_K):
            phase = (first + j) % SUBLANES
            base = first + j - phase + r0
            rows = ext_ref[base:base + chunk, :] if phase == 0 else sh_ref[phase - 1, base:base + chunk, :]
            acc = acc + rows.reshape(tiles, SUBLANES, CONV_DIM) * cw_ref[j]
        conv_ref[r0:r0 + chunk, :] = acc.reshape(chunk, CONV_DIM)
        if jobs:
            jobs.pop(0)()
    while jobs:
        jobs.pop(0)()

    c = _layer_norm(conv_ref[...], clg_ref[...], clb_ref[...])
    c = c * jax.nn.sigmoid(c)
    branch_b = jnp.dot(c.astype(BF16), wbc_ref[...], preferred_element_type=F32)
    x = x_ref[...]
    m = gates_ref[:, :D_MODEL] * ba_ref[...] + gates_ref[:, D_MODEL:] * branch_b
    mixed = jnp.dot(m.astype(BF16), wo_ref[...], preferred_element_type=F32)
    h = _layer_norm(alpha * x + mixed, l1g_ref[...], l1b_ref[...])
    hc_ref[:, :D_MODEL] = h

    logits = jnp.dot(h.astype(BF16), wr_ref[...], preferred_element_type=F32) + br_ref[...]
    lane = lax.broadcasted_iota(jnp.int32, (tm, ROUTER_LANES), 1)
    neg = jnp.float32(-jnp.inf)
    gl = jnp.where(lane < N_GROUPS, logits, neg)
    gmax = jnp.max(gl, axis=-1, keepdims=True)
    g_idx = jnp.min(jnp.where(gl == gmax, lane, ROUTER_LANES), axis=-1, keepdims=True)
    p_group = 1.0 / jnp.sum(jnp.exp(gl - gmax), axis=-1, keepdims=True)
    lo = N_GROUPS + EXPERTS_PER_GROUP * g_idx
    el = jnp.where((lane >= lo) & (lane < lo + EXPERTS_PER_GROUP), logits, neg)
    top1 = jnp.max(el, axis=-1, keepdims=True)
    i1 = jnp.min(jnp.where(el == top1, lane, ROUTER_LANES), axis=-1, keepdims=True)
    el2 = jnp.where(lane == i1, neg, el)
    top2 = jnp.max(el2, axis=-1, keepdims=True)
    i2 = jnp.min(jnp.where(el2 == top2, lane, ROUTER_LANES), axis=-1, keepdims=True)
    e2 = jnp.exp(top2 - top1)
    p1 = 1.0 / (1.0 + e2)
    p2 = e2 / (1.0 + e2)
    comb = (jnp.where(lane == i1, p_group * p1, 0.0)
            + jnp.where(lane == i2, p_group * p2, 0.0))

    @pl.when((pl.program_id(0) == 0) & (i == 0))
    def _():
        cnt_ref[...] = jnp.zeros_like(cnt_ref)

    member = (lane == g_idx).astype(F32)
    before = jnp.dot(lt_ref[...], member.astype(BF16), preferred_element_type=F32) + cnt_ref[0:1, :]
    rank = jnp.sum(member * before, axis=-1, keepdims=True)
    cnt_ref[...] = cnt_ref[...] + jnp.sum(member, axis=0, keepdims=True)
    hc_ref[:, D_MODEL:] = jnp.where(lane == 0, g_idx.astype(F32), jnp.where(lane == 1, rank, comb))


def _post(x2d, attn, u, hist, lw, *, bsz, tlen, tm, alpha):
    n = x2d.shape[0]
    nt = tlen // tm
    row = lambda b, i: (b * nt + i, 0)
    const = lambda b, i: (0, 0)

    def halo_map(b, i):
        return (jnp.maximum((b * tlen + i * tm) // HALO - 1, 0), 0)

    full = lambda a: pl.BlockSpec(a.shape, lambda b, i, nd=a.ndim: (0,) * nd)
    r = lax.broadcasted_iota(jnp.int32, (tm, tm), 0)
    c = lax.broadcasted_iota(jnp.int32, (tm, tm), 1)
    earlier = jnp.where(c < r, 1.0, 0.0).astype(BF16)
    weights = [lw["wba"], lw["wbc"], lw["conv_w"], lw["conv_b"], lw["conv_ln_g"], lw["conv_ln_b"],
               lw["w_gate"], lw["b_gate"], lw["w_out"], lw["ln1_g"], lw["ln1_b"],
               lw["w_router"], lw["b_router"], earlier]
    return pl.pallas_call(
        functools.partial(_post_kernel, tm=tm, alpha=alpha),
        grid=(bsz, nt),
        in_specs=[pl.BlockSpec((tm, D_MODEL), row),
                  pl.BlockSpec((tm, SB_DIM), row),
                  pl.BlockSpec((tm, CONV_DIM), row),
                  pl.BlockSpec((HALO, CONV_DIM), halo_map),
                  pl.BlockSpec((HALO, CONV_DIM), lambda b, i: (b, 0))] + [full(w) for w in weights],
        out_specs=[pl.BlockSpec((tm, ROW_WIDTH), row),
                   pl.BlockSpec((SUBLANES, ROUTER_LANES), lambda b, i: (0, 0))],
        out_shape=[jax.ShapeDtypeStruct((n, ROW_WIDTH), F32),
                   jax.ShapeDtypeStruct((SUBLANES, ROUTER_LANES), F32)],
        scratch_shapes=[pltpu.VMEM((HALO + tm, CONV_DIM), F32),
                        pltpu.VMEM((SUBLANES - 1, HALO + tm - SUBLANES, CONV_DIM), F32),
                        pltpu.VMEM((tm, CONV_DIM), F32),
                        pltpu.VMEM((tm, D_MODEL), BF16),
                        pltpu.VMEM((tm, 2 * D_MODEL), F32),
                        pltpu.VMEM((tm, D_MODEL), F32)],
        compiler_params=pltpu.CompilerParams(dimension_semantics=("arbitrary", "arbitrary"),
                                             vmem_limit_bytes=VMEM_LIMIT),
        name="post",
    )(x2d, attn, u, u, hist, *weights)


def _combine_weight(meta, lane_id):
    lane = lax.broadcasted_iota(jnp.int32, meta.shape, 1)
    return jnp.sum(jnp.where(lane == lane_id, meta, 0.0), axis=-1, keepdims=True)


def _moe_dense_kernel(hc_ref, wg_ref, wu_ref, wd_ref, l2g_ref, l2b_ref, y_ref,
                      hb_ref, acc_ref, *, alpha):
    e = pl.program_id(1)

    @pl.when(e == 0)
    def _():
        hb_ref[...] = hc_ref[:, :D_MODEL].astype(BF16)
        acc_ref[...] = jnp.zeros_like(acc_ref)

    hb = hb_ref[...]
    gate = jnp.dot(hb, wg_ref[0], preferred_element_type=F32)
    up = jnp.dot(hb, wu_ref[0], preferred_element_type=F32)
    c_e = _combine_weight(hc_ref[:, D_MODEL:], N_GROUPS + e)
    hid = gate * jax.nn.sigmoid(gate) * up * c_e
    acc_ref[...] += jnp.dot(hid.astype(BF16), wd_ref[0], preferred_element_type=F32)

    @pl.when(e == N_EXPERTS - 1)
    def _():
        y_ref[...] = _layer_norm(alpha * hc_ref[:, :D_MODEL] + acc_ref[...],
                                 l2g_ref[...], l2b_ref[...])


def _moe_dense(hc, lw, *, tm, alpha):
    n = hc.shape[0]
    row = lambda i, e: (i, 0)
    const = lambda i, e: (0, 0)
    expert = lambda i, e: (e, 0, 0)
    return pl.pallas_call(
        functools.partial(_moe_dense_kernel, alpha=alpha),
        grid=(n // tm, N_EXPERTS),
        in_specs=[pl.BlockSpec((tm, ROW_WIDTH), row),
                  pl.BlockSpec((1, D_MODEL, EXPERT_DFF), expert),
                  pl.BlockSpec((1, D_MODEL, EXPERT_DFF), expert),
                  pl.BlockSpec((1, EXPERT_DFF, D_MODEL), expert),
                  pl.BlockSpec((1, D_MODEL), const),
                  pl.BlockSpec((1, D_MODEL), const)],
        out_specs=pl.BlockSpec((tm, D_MODEL), row),
        out_shape=jax.ShapeDtypeStruct((n, D_MODEL), F32),
        scratch_shapes=[pltpu.VMEM((tm, D_MODEL), BF16), pltpu.VMEM((tm, D_MODEL), F32)],
        compiler_params=pltpu.CompilerParams(dimension_semantics=("arbitrary", "arbitrary"),
                                             vmem_limit_bytes=VMEM_LIMIT),
        name="moe_dense",
    )(hc, lw["w_expert_gate"], lw["w_expert_up"], lw["w_expert_down"], lw["ln2_g"], lw["ln2_b"])


def _row_copy(src_ref, src_row, dst_ref, dst_row, sem):
    return pltpu.make_async_copy(src_ref.at[pl.ds(src_row, 1)], dst_ref.at[pl.ds(dst_row, 1)], sem)


def _scatter_kernel(pad_start_ref, pad_len_ref, hc_ref, dst_ref, hs_ref, zero_ref, sem, *, rows):
    @pl.when(pl.program_id(0) == 0)
    def _():
        zero_ref[...] = jnp.zeros_like(zero_ref)
        for g in range(N_GROUPS):
            def fill(k, _):
                _row_copy(zero_ref, 0, hs_ref, pad_start_ref[g] + k, sem).start()
                return 0

            def drain(k, _):
                _row_copy(zero_ref, 0, hs_ref, pad_start_ref[g] + k, sem).wait()
                return 0

            lax.fori_loop(0, pad_len_ref[g], fill, 0)
            lax.fori_loop(0, pad_len_ref[g], drain, 0)

    def start(r, _):
        _row_copy(hc_ref, r, hs_ref, dst_ref[0, 0, r], sem).start()
        return 0

    def wait(r, _):
        _row_copy(hc_ref, r, hs_ref, dst_ref[0, 0, r], sem).wait()
        return 0

    lax.fori_loop(0, rows, start, 0, unroll=8)
    lax.fori_loop(0, rows, wait, 0, unroll=8)


def _gather_kernel(ys_ref, dst_ref, y_ref, sem, *, rows):
    def start(r, _):
        _row_copy(ys_ref, dst_ref[0, 0, r], y_ref, r, sem).start()
        return 0

    def wait(r, _):
        _row_copy(ys_ref, dst_ref[0, 0, r], y_ref, r, sem).wait()
        return 0

    lax.fori_loop(0, rows, start, 0, unroll=8)
    lax.fori_loop(0, rows, wait, 0, unroll=8)


def _grouped_kernel(group_ref, used_ref, hs_ref, wg_ref, wu_ref, wd_ref, l2g_ref, l2b_ref, ys_ref,
                    *, alpha):
    t = pl.program_id(0)

    @pl.when(t < used_ref[0])
    def _():
        h = hs_ref[:, :D_MODEL]
        meta = hs_ref[:, D_MODEL:]
        hb = h.astype(BF16)
        gate = jnp.dot(hb, wg_ref[0], preferred_element_type=F32)
        up = jnp.dot(hb, wu_ref[0], preferred_element_type=F32)
        hid = gate * jax.nn.sigmoid(gate) * up
        first = N_GROUPS + EXPERTS_PER_GROUP * group_ref[t]
        scaled = [hid[:, e * EXPERT_DFF:(e + 1) * EXPERT_DFF] * _combine_weight(meta, first + e)
                  for e in range(EXPERTS_PER_GROUP)]
        ff = jnp.dot(jnp.concatenate(scaled, axis=1).astype(BF16), wd_ref[0],
                     preferred_element_type=F32)
        ys_ref[...] = _layer_norm(alpha * h + ff, l2g_ref[...], l2b_ref[...])

    @pl.when(t >= used_ref[0])
    def _():
        ys_ref[...] = jnp.zeros_like(ys_ref)


def _moe_sparse(hc, counts, lw, *, alpha):
    n = hc.shape[0]
    n_tiles = n // MOE_TILE + N_GROUPS
    n_sorted = n_tiles * MOE_TILE
    counts = counts.astype(jnp.int32)
    tiles = (counts + MOE_TILE - 1) // MOE_TILE
    tile_end = jnp.cumsum(tiles)
    first_row = (tile_end - tiles) * MOE_TILE
    group = hc[:, D_MODEL].astype(jnp.int32)
    rank = hc[:, D_MODEL + 1].astype(jnp.int32)
    dst = (first_row[group] + rank).reshape(n // MOVE_TILE, 1, MOVE_TILE)
    tile_group = jnp.minimum(
        jnp.sum(jnp.arange(n_tiles, dtype=jnp.int32)[:, None] >= tile_end[None, :], axis=1),
        N_GROUPS - 1).astype(jnp.int32)
    used = tile_end[N_GROUPS - 1:].astype(jnp.int32)
    pad_start = (first_row + counts).astype(jnp.int32)
    next_start = jnp.concatenate([first_row[1:], jnp.full((1,), n_sorted, jnp.int32)])
    pad_len = (next_start - pad_start).astype(jnp.int32)

    moved = pltpu.CompilerParams(dimension_semantics=("arbitrary",), vmem_limit_bytes=VMEM_LIMIT)
    hs = pl.pallas_call(
        functools.partial(_scatter_kernel, rows=MOVE_TILE),
        grid_spec=pltpu.PrefetchScalarGridSpec(
            num_scalar_prefetch=2,
            grid=(n // MOVE_TILE,),
            in_specs=[pl.BlockSpec((MOVE_TILE, ROW_WIDTH), lambda i, ps, pn: (i, 0)),
                      pl.BlockSpec((1, 1, MOVE_TILE), lambda i, ps, pn: (i, 0, 0),
                                   memory_space=pltpu.SMEM)],
            out_specs=pl.BlockSpec(memory_space=pl.ANY),
            scratch_shapes=[pltpu.VMEM((SUBLANES, ROW_WIDTH), F32), pltpu.SemaphoreType.DMA(())]),
        out_shape=jax.ShapeDtypeStruct((n_sorted, ROW_WIDTH), F32),
        compiler_params=moved,
        name="moe_scatter",
    )(pad_start, pad_len, hc, dst)

    group_w = lambda t, grp, use: (grp[t], 0, 0)
    ys = pl.pallas_call(
        functools.partial(_grouped_kernel, alpha=alpha),
        grid_spec=pltpu.PrefetchScalarGridSpec(
            num_scalar_prefetch=2,
            grid=(n_tiles,),
            in_specs=[pl.BlockSpec((MOE_TILE, ROW_WIDTH), lambda t, grp, use: (t, 0)),
                      pl.BlockSpec((1, D_MODEL, D_MODEL), group_w),
                      pl.BlockSpec((1, D_MODEL, D_MODEL), group_w),
                      pl.BlockSpec((1, D_MODEL, D_MODEL), group_w),
                      pl.BlockSpec((1, D_MODEL), lambda t, grp, use: (0, 0)),
                      pl.BlockSpec((1, D_MODEL), lambda t, grp, use: (0, 0))],
            out_specs=pl.BlockSpec((MOE_TILE, D_MODEL), lambda t, grp, use: (t, 0))),
        out_shape=jax.ShapeDtypeStruct((n_sorted, D_MODEL), F32),
        compiler_params=moved,
        name="moe_grouped",
    )(tile_group, used, hs, lw["w_group_gate"], lw["w_group_up"], lw["w_group_down"],
      lw["ln2_g"], lw["ln2_b"])

    return pl.pallas_call(
        functools.partial(_gather_kernel, rows=MOVE_TILE),
        grid=(n // MOVE_TILE,),
        in_specs=[pl.BlockSpec(memory_space=pl.ANY),
                  pl.BlockSpec((1, 1, MOVE_TILE), lambda i: (i, 0, 0), memory_space=pltpu.SMEM)],
        out_specs=pl.BlockSpec((MOVE_TILE, D_MODEL), lambda i: (i, 0)),
        out_shape=jax.ShapeDtypeStruct((n, D_MODEL), F32),
        scratch_shapes=[pltpu.SemaphoreType.DMA(())],
        compiler_params=moved,
        name="moe_gather",
    )(ys, dst)


def _encoder_layer(x, k_hist, v_hist, conv_hist, lw, *, alpha, proj_tm, post_tm, moe_tm, tq, pairs):
    bsz, tlen, _ = x.shape
    n = bsz * tlen
    x2d = x.reshape(n, D_MODEL)
    q, k, v, kb, vb, u = _proj(x2d, lw["w_in"], proj_tm)

    nq = tlen // tq
    if k_hist is None:
        kn, vn, new_rows, ck, cv = kb, vb, tlen, None, None
    else:
        pad = lambda a: jnp.pad(a.reshape(bsz, tlen, SB_DIM),
                                ((0, 0), (0, KEY_TILE - tlen), (0, 0))).reshape(bsz * KEY_TILE, SB_DIM)
        kn, vn, new_rows = pad(kb), pad(vb), KEY_TILE
        past = k_hist.shape[1]
        ck = k_hist.reshape(bsz * past, SB_DIM)
        cv = v_hist.reshape(bsz * past, SB_DIM)
    attn = _attention(q, kn, vn, ck, cv, bsz=bsz, tq=tq, nq=nq, new_rows=new_rows, pairs=pairs)

    if conv_hist is None:
        hist = jnp.zeros((bsz * HALO, CONV_DIM), F32)
    else:
        hist = jnp.pad(conv_hist, ((0, 0), (HALO - (CONV_K - 1), 0), (0, 0))).reshape(bsz * HALO, CONV_DIM)
    hc, counts = _post(x2d, attn, u, hist, lw, bsz=bsz, tlen=tlen, tm=post_tm, alpha=alpha)
    if n % MOVE_TILE == 0:
        y = _moe_sparse(hc, counts[0, :N_GROUPS], lw, alpha=alpha)
    else:
        y = _moe_dense(hc, lw, tm=moe_tm, alpha=alpha)

    u3 = u.reshape(bsz, tlen, CONV_DIM)
    if conv_hist is None:
        conv_state = u3[:, tlen - (CONV_K - 1):]
    else:
        conv_state = jnp.concatenate([conv_hist, u3], axis=1)[:, -(CONV_K - 1):]
    return (y.reshape(bsz, tlen, D_MODEL),
            k.reshape(bsz, tlen, SB_HEADS, SB_HEAD_DIM),
            v.reshape(bsz, tlen, SB_HEADS, SB_HEAD_DIM),
            conv_state)


def _by_group_cols(w):
    w = w.astype(BF16).reshape(N_GROUPS, EXPERTS_PER_GROUP, D_MODEL, EXPERT_DFF)
    return w.transpose(0, 2, 1, 3).reshape(N_GROUPS, D_MODEL, EXPERTS_PER_GROUP * EXPERT_DFF)


def _layer_weights(l, w_in, w_branch_attn, w_branch_conv, conv_w, conv_b, conv_ln_g, conv_ln_b,
                   w_gate, b_gate, w_out, ln1_g, ln1_b, w_router_group, b_router_group,
                   w_router_expert, b_router_expert, w_expert_gate, w_expert_up, w_expert_down,
                   ln2_g, ln2_b):
    n_route = N_GROUPS + N_EXPERTS
    w_router = jnp.concatenate([w_router_group[l], w_router_expert[l]], axis=1)
    w_router = jnp.pad(w_router, ((0, 0), (0, ROUTER_LANES - n_route)))
    b_router = jnp.pad(jnp.concatenate([b_router_group[l], b_router_expert[l]]),
                       (0, ROUTER_LANES - n_route))
    row = lambda a: a.reshape(1, -1).astype(F32)
    return dict(
        w_in=w_in[l].astype(BF16), wba=w_branch_attn[l].astype(BF16),
        wbc=w_branch_conv[l].astype(BF16),
        conv_w=jnp.broadcast_to(conv_w[l].astype(F32)[:, None, :], (CONV_K, SUBLANES, CONV_DIM)),
        conv_b=row(conv_b[l]),
        conv_ln_g=row(conv_ln_g[l]), conv_ln_b=row(conv_ln_b[l]),
        w_gate=w_gate[l].astype(BF16), b_gate=row(b_gate[l]), w_out=w_out[l].astype(BF16),
        ln1_g=row(ln1_g[l]), ln1_b=row(ln1_b[l]),
        w_router=w_router.astype(BF16), b_router=row(b_router),
        w_expert_gate=w_expert_gate[l].astype(BF16), w_expert_up=w_expert_up[l].astype(BF16),
        w_expert_down=w_expert_down[l].astype(BF16), ln2_g=row(ln2_g[l]), ln2_b=row(ln2_b[l]),
        w_group_gate=_by_group_cols(w_expert_gate[l]), w_group_up=_by_group_cols(w_expert_up[l]),
        w_group_down=w_expert_down[l].astype(BF16).reshape(
            N_GROUPS, EXPERTS_PER_GROUP * EXPERT_DFF, D_MODEL))


def kernel(x_prompt, x_sample, cache_k, cache_v, state_conv, w_in, w_branch_attn, w_branch_conv, conv_w, conv_b, conv_ln_g, conv_ln_b, w_gate, b_gate, w_out, ln1_g, ln1_b, w_router_group, b_router_group, w_router_expert, b_router_expert, w_expert_gate, w_expert_up, w_expert_down, ln2_g, ln2_b):
    depth = w_in.shape[0]
    alpha = (2.0 * depth) ** 0.25
    h_p, h_s = x_prompt, x_sample
    outs = [[] for _ in range(6)]
    for l in range(depth):
        lw = _layer_weights(l, w_in, w_branch_attn, w_branch_conv, conv_w, conv_b, conv_ln_g,
                            conv_ln_b, w_gate, b_gate, w_out, ln1_g, ln1_b, w_router_group,
                            b_router_group, w_router_expert, b_router_expert, w_expert_gate,
                            w_expert_up, w_expert_down, ln2_g, ln2_b)
        n_p = h_p.shape[0] * h_p.shape[1]
        h_p, k_p, v_p, c_p = _encoder_layer(
            h_p, None, None, None, lw, alpha=alpha,
            proj_tm=min(512, n_p), post_tm=min(512, h_p.shape[1]), moe_tm=min(1024, n_p),
            tq=128, pairs=4)
        ts = h_s.shape[1]
        h_s, k_s, v_s, c_s = _encoder_layer(
            h_s, cache_k[l], cache_v[l], state_conv[l], lw, alpha=alpha,
            proj_tm=h_s.shape[0] * ts, post_tm=ts, moe_tm=h_s.shape[0] * ts, tq=ts, pairs=1)
        for lst, val in zip(outs, (k_p, v_p, c_p, k_s, v_s, c_s)):
            lst.append(val)
    return (h_p, h_s) + tuple(jnp.stack(o) for o in outs)
```

```python
import functools

import jax
import jax.numpy as jnp
from jax import lax
from jax.experimental import pallas as pl
from jax.experimental.pallas import tpu as pltpu

F32 = jnp.float32
BF16 = jnp.bfloat16

D_MODEL = 1024
SB_DIM = 512
SB_HEADS = 8
SB_HEAD_DIM = 64
SB_SCALE = SB_HEAD_DIM ** -0.5
CONV_DIM = 512
CONV_K = 31
N_GROUPS = 4
EXPERTS_PER_GROUP = 4
N_EXPERTS = 16
EXPERT_DFF = 256
LN_EPS = 1e-5
IN_DIM = 3 * SB_DIM + 2 * CONV_DIM

LANES = 128
SUBLANES = 8
CONV_CHUNK = 32
MXU_COLS = 256
KEY_TILE = 128
CACHE_WINDOW_TILES = 4
HALO = 32
ROUTER_LANES = 128
ROW_WIDTH = D_MODEL + ROUTER_LANES
MOE_TILE = 256
MOVE_TILE = 2048
VMEM_LIMIT = 56 * 1024 * 1024
EXP_UNDERFLOW = -88.0
MASKED_LOG_WEIGHT = -1e30


def _layer_norm(x, g, b):
    mu = jnp.mean(x, axis=-1, keepdims=True)
    xc = x - mu
    var = jnp.mean(xc * xc, axis=-1, keepdims=True)
    return xc * lax.rsqrt(var + LN_EPS) * g + b


def _proj_kernel(x_ref, w_ref, q_ref, k_ref, v_ref, kb_ref, vb_ref, u_ref):
    xb = x_ref[...].astype(BF16)

    def mm(lo, hi):
        return jnp.dot(xb, w_ref[:, lo:hi], preferred_element_type=F32)

    q_ref[...] = (mm(0, SB_DIM) * SB_SCALE).astype(BF16)
    k = mm(SB_DIM, 2 * SB_DIM)
    k_ref[...] = k
    kb_ref[...] = k.astype(BF16)
    v = mm(2 * SB_DIM, 3 * SB_DIM)
    v_ref[...] = v
    vb_ref[...] = v.astype(BF16)
    val = mm(3 * SB_DIM, 3 * SB_DIM + CONV_DIM)
    gate = mm(3 * SB_DIM + CONV_DIM, IN_DIM)
    u_ref[...] = val * jax.nn.sigmoid(gate)


def _proj(x2d, w_in_bf, tm):
    n = x2d.shape[0]
    row = lambda i: (i, 0)
    outs = [jax.ShapeDtypeStruct((n, SB_DIM), BF16),
            jax.ShapeDtypeStruct((n, SB_DIM), F32),
            jax.ShapeDtypeStruct((n, SB_DIM), F32),
            jax.ShapeDtypeStruct((n, SB_DIM), BF16),
            jax.ShapeDtypeStruct((n, SB_DIM), BF16),
            jax.ShapeDtypeStruct((n, CONV_DIM), F32)]
    return pl.pallas_call(
        _proj_kernel,
        grid=(n // tm,),
        in_specs=[pl.BlockSpec((tm, D_MODEL), row),
                  pl.BlockSpec((D_MODEL, IN_DIM), lambda i: (0, 0))],
        out_specs=[pl.BlockSpec((tm, SB_DIM), row)] * 5 + [pl.BlockSpec((tm, CONV_DIM), row)],
        out_shape=outs,
        compiler_params=pltpu.CompilerParams(dimension_semantics=("arbitrary",),
                                             vmem_limit_bytes=VMEM_LIMIT),
        name="proj",
    )(x2d, w_in_bf)


def _attn_kernel(*refs, tq, pairs, n_cache_tiles, window_tiles):
    if n_cache_tiles:
        (q_ref, kn_ref, vn_ref, kwin_ref, vwin_ref, kc_hbm, vc_hbm, uu_ref, o_ref,
         qs_ref, acc_ref, run_ref, zs_ref, vs_ref, kst_ref, vst_ref, kdeep_ref, vdeep_ref,
         sem) = refs
    else:
        q_ref, kn_ref, vn_ref, uu_ref, o_ref, qs_ref, acc_ref, run_ref, zs_ref, vs_ref = refs
    i = pl.program_id(2)
    rows = 2 * tq
    uu = uu_ref[...]

    lane_q = lax.broadcasted_iota(jnp.int32, (tq, LANES), 1)
    lane_k = lax.broadcasted_iota(jnp.int32, (KEY_TILE, LANES), 1)
    r_idx = lax.broadcasted_iota(jnp.int32, (rows, KEY_TILE), 0)
    r_idx = jnp.where(r_idx >= tq, r_idx - tq, r_idx)
    c_idx = lax.broadcasted_iota(jnp.int32, (rows, KEY_TILE), 1)
    causal = c_idx < r_idx

    def score_sweep(slot, load_kv, mask, first):
        zs, splits = [], []
        for p in range(pairs):
            kt, vt = load_kv(slice(p * LANES, (p + 1) * LANES))
            vs_ref[slot, p, 0] = jnp.where(lane_k < SB_HEAD_DIM, vt, jnp.zeros_like(vt))
            vs_ref[slot, p, 1] = jnp.where(lane_k >= SB_HEAD_DIM, vt, jnp.zeros_like(vt))
            z = lax.dot_general(qs_ref[p], kt, (((1,), (1,)), ((), ())),
                                preferred_element_type=F32)
            lk = -(jnp.maximum(z, 0.0) + jnp.log(1.0 + jnp.exp(-jnp.abs(z))))
            if mask is not None:
                lk = jnp.where(mask, lk, 0.0)
            hi = lk.astype(BF16)
            lo = (lk - hi.astype(F32)).astype(BF16)
            zs.append(z)
            splits.append(jnp.concatenate([hi, lo], axis=1))
        su = jnp.dot(jnp.concatenate(splits, axis=0), uu, preferred_element_type=F32)
        worst = None
        for p in range(pairs):
            sp = su[p * rows:(p + 1) * rows]
            logw = zs[p] + sp[:, :KEY_TILE]
            tot = sp[:, KEY_TILE:]
            if not first:
                run = run_ref[p]
                logw = logw + run
                tot = tot + run
            if mask is not None:
                logw = jnp.where(mask, logw, MASKED_LOG_WEIGHT)
            zs_ref[slot, p] = logw
            run_ref[p] = tot
            worst = tot if worst is None else jnp.maximum(worst, tot)
        return jnp.max(worst)

    def value_sweep(slot):
        for p in range(pairs):
            wb = jnp.exp(zs_ref[slot, p]).astype(BF16)
            acc_ref[p] += (jnp.dot(wb[:tq], vs_ref[slot, p, 0], preferred_element_type=F32)
                           + jnp.dot(wb[tq:], vs_ref[slot, p, 1], preferred_element_type=F32))

    for p in range(pairs):
        qp = q_ref[:, p * LANES:(p + 1) * LANES]
        qs_ref[p] = jnp.concatenate(
            [jnp.where(lane_q < SB_HEAD_DIM, qp, jnp.zeros_like(qp)),
             jnp.where(lane_q >= SB_HEAD_DIM, qp, jnp.zeros_like(qp))], axis=0)

    def new_tile(j):
        off = pl.multiple_of(j * KEY_TILE, KEY_TILE)
        return lambda cols: (kn_ref[pl.ds(off, KEY_TILE), cols], vn_ref[pl.ds(off, KEY_TILE), cols])

    def stage_heads(src, off, dst_ref):
        heads = [src[pl.ds(off, KEY_TILE), hd, :] for hd in range(SB_HEADS)]
        dst_ref[...] = jnp.concatenate(heads, axis=1).astype(BF16)

    def cache_tile(j):
        first_window_tile = n_cache_tiles - window_tiles

        @pl.when(j >= first_window_tile)
        def _():
            off = pl.multiple_of((j - first_window_tile) * KEY_TILE, KEY_TILE)
            stage_heads(kwin_ref.at[0], off, kst_ref)
            stage_heads(vwin_ref.at[0], off, vst_ref)

        @pl.when(j < first_window_tile)
        def _():
            b = pl.program_id(0)
            rows = pl.ds(pl.multiple_of(j * KEY_TILE, KEY_TILE), KEY_TILE)
            fetch_k = pltpu.make_async_copy(kc_hbm.at[b, rows], kdeep_ref, sem.at[0])
            fetch_v = pltpu.make_async_copy(vc_hbm.at[b, rows], vdeep_ref, sem.at[1])
            fetch_k.start()
            fetch_v.start()
            fetch_k.wait()
            fetch_v.wait()
            stage_heads(kdeep_ref, 0, kst_ref)
            stage_heads(vdeep_ref, 0, vst_ref)

        return lambda cols: (kst_ref[:, cols], vst_ref[:, cols])

    live = lambda c: (c[0] >= 0) & (c[2] > EXP_UNDERFLOW)

    def pipelined(tile_of):
        def body(c):
            j, slot, _ = c
            value_sweep(slot)
            return j - 1, 1 - slot, score_sweep(1 - slot, tile_of(j), None, False)
        return body

    worst = score_sweep(0, new_tile(i), causal, True)
    acc_ref[...] = jnp.zeros_like(acc_ref)
    carry = lax.while_loop(live, pipelined(new_tile), (i - 1, jnp.int32(0), worst))
    if n_cache_tiles:
        carry = lax.while_loop(live, pipelined(cache_tile),
                               (jnp.int32(n_cache_tiles - 1), carry[1], carry[2]))
    value_sweep(carry[1])

    for p in range(pairs):
        o_ref[:, p * LANES:(p + 1) * LANES] = acc_ref[p].astype(BF16)


def _suffix_sum_matrix():
    r = lax.broadcasted_iota(jnp.int32, (2 * KEY_TILE, 2 * KEY_TILE), 0) % KEY_TILE
    c = lax.broadcasted_iota(jnp.int32, (2 * KEY_TILE, 2 * KEY_TILE), 1)
    return jnp.where((c >= KEY_TILE) | (r >= c), 1.0, 0.0).astype(BF16)


def _attention(q, kn, vn, cache_k, cache_v, *, bsz, tq, nq, new_rows, pairs):
    n = q.shape[0]
    width = pairs * LANES
    steps = SB_DIM // width
    n_cache_tiles = 0 if cache_k is None else cache_k.shape[1] // KEY_TILE
    window_tiles = 0
    in_specs = [pl.BlockSpec((tq, width), lambda b, p, i: (b * nq + i, p)),
                pl.BlockSpec((new_rows, width), lambda b, p, i: (b, p)),
                pl.BlockSpec((new_rows, width), lambda b, p, i: (b, p))]
    args = [q, kn, vn]
    scratch = [pltpu.VMEM((pairs, 2 * tq, LANES), BF16),
               pltpu.VMEM((pairs, tq, LANES), F32),
               pltpu.VMEM((pairs, 2 * tq, KEY_TILE), F32),
               pltpu.VMEM((2, pairs, 2 * tq, KEY_TILE), F32),
               pltpu.VMEM((2, pairs, 2, KEY_TILE, LANES), BF16)]
    if n_cache_tiles:
        assert steps == 1 and cache_k.shape[1] % KEY_TILE == 0
        window_tiles = CACHE_WINDOW_TILES if n_cache_tiles % CACHE_WINDOW_TILES == 0 else 1
        window = (1, window_tiles * KEY_TILE, SB_HEADS, SB_HEAD_DIM)
        last = n_cache_tiles // window_tiles - 1
        in_specs += [pl.BlockSpec(window, lambda b, p, i: (b, last, 0, 0))] * 2
        in_specs += [pl.BlockSpec(memory_space=pl.ANY)] * 2
        args += [cache_k, cache_v, cache_k, cache_v]
        scratch += [pltpu.VMEM((KEY_TILE, SB_DIM), BF16),
                    pltpu.VMEM((KEY_TILE, SB_DIM), BF16),
                    pltpu.VMEM((KEY_TILE, SB_HEADS, SB_HEAD_DIM), F32),
                    pltpu.VMEM((KEY_TILE, SB_HEADS, SB_HEAD_DIM), F32),
                    pltpu.SemaphoreType.DMA((2,))]
    in_specs.append(pl.BlockSpec((2 * KEY_TILE, 2 * KEY_TILE), lambda b, p, i: (0, 0)))
    args.append(_suffix_sum_matrix())
    return pl.pallas_call(
        functools.partial(_attn_kernel, tq=tq, pairs=pairs, n_cache_tiles=n_cache_tiles,
                          window_tiles=window_tiles),
        grid=(bsz, steps, nq),
        in_specs=in_specs,
        out_specs=pl.BlockSpec((tq, width), lambda b, p, i: (b * nq + i, p)),
        out_shape=jax.ShapeDtypeStruct((n, SB_DIM), BF16),
        scratch_shapes=scratch,
        compiler_params=pltpu.CompilerParams(
            dimension_semantics=("arbitrary", "arbitrary", "arbitrary"),
            vmem_limit_bytes=VMEM_LIMIT),
        name="attn",
    )(*args)


def _post_kernel(x_ref, attn_ref, u_ref, halo_ref, hist_ref, wba_ref, wbc_ref, cw_ref, cb_ref,
                 clg_ref, clb_ref, wg_ref, bg_ref, wo_ref, l1g_ref, l1b_ref, wr_ref, br_ref,
                 lt_ref, hc_ref, cnt_ref, ext_ref, sh_ref, conv_ref, xb_ref, gates_ref, ba_ref,
                 *, tm, alpha):
    i = pl.program_id(1)
    prev = jnp.where(i == 0, hist_ref[...], halo_ref[...])
    ext_ref[0:HALO, :] = prev
    ext_ref[HALO:HALO + tm, :] = u_ref[...]
    span = tm + HALO - SUBLANES
    for s in range(1, SUBLANES):
        sh_ref[s - 1] = ext_ref[s:s + span, :]
    xb_ref[...] = x_ref[...].astype(BF16)

    def gate_block(n):
        cols = slice(n * MXU_COLS, (n + 1) * MXU_COLS)
        gates_ref[:, cols] = jax.nn.sigmoid(
            jnp.dot(xb_ref[...], wg_ref[:, cols], preferred_element_type=F32) + bg_ref[:, cols])

    def attn_block(n):
        cols = slice(n * MXU_COLS, (n + 1) * MXU_COLS)
        ba_ref[:, cols] = jnp.dot(attn_ref[...], wba_ref[:, cols], preferred_element_type=F32)

    jobs = ([functools.partial(gate_block, n) for n in range(2 * D_MODEL // MXU_COLS)]
            + [functools.partial(attn_block, n) for n in range(D_MODEL // MXU_COLS)])

    first = HALO - (CONV_K - 1)
    chunk = min(CONV_CHUNK, tm)
    tiles = chunk // SUBLANES
    for idx, r0 in enumerate(range(0, tm, chunk)):
        acc = jnp.zeros((tiles, SUBLANES, CONV_DIM), F32) + cb_ref[...]
        for j in range(CONV_K):
            phase = (first + j) % SUBLANES
            base = first + j - phase + r0
            rows = ext_ref[base:base + chunk, :] if phase == 0 else sh_ref[phase - 1, base:base + chunk, :]
            acc = acc + rows.reshape(tiles, SUBLANES, CONV_DIM) * cw_ref[j]
        conv_ref[r0:r0 + chunk, :] = acc.reshape(chunk, CONV_DIM)
        if jobs:
            jobs.pop(0)()
    while jobs:
        jobs.pop(0)()

    c = _layer_norm(conv_ref[...], clg_ref[...], clb_ref[...])
    c = c * jax.nn.sigmoid(c)
    branch_b = jnp.dot(c.astype(BF16), wbc_ref[...], preferred_element_type=F32)
    x = x_ref[...]
    m = gates_ref[:, :D_MODEL] * ba_ref[...] + gates_ref[:, D_MODEL:] * branch_b
    mixed = jnp.dot(m.astype(BF16), wo_ref[...], preferred_element_type=F32)
    h = _layer_norm(alpha * x + mixed, l1g_ref[...], l1b_ref[...])
    hc_ref[:, :D_MODEL] = h

    logits = jnp.dot(h.astype(BF16), wr_ref[...], preferred_element_type=F32) + br_ref[...]
    lane = lax.broadcasted_iota(jnp.int32, (tm, ROUTER_LANES), 1)
    neg = jnp.float32(-jnp.inf)
    gl = jnp.where(lane < N_GROUPS, logits, neg)
    gmax = jnp.max(gl, axis=-1, keepdims=True)
    g_idx = jnp.min(jnp.where(gl == gmax, lane, ROUTER_LANES), axis=-1, keepdims=True)
    p_group = 1.0 / jnp.sum(jnp.exp(gl - gmax), axis=-1, keepdims=True)
    lo = N_GROUPS + EXPERTS_PER_GROUP * g_idx
    el = jnp.where((lane >= lo) & (lane < lo + EXPERTS_PER_GROUP), logits, neg)
    top1 = jnp.max(el, axis=-1, keepdims=True)
    i1 = jnp.min(jnp.where(el == top1, lane, ROUTER_LANES), axis=-1, keepdims=True)
    el2 = jnp.where(lane == i1, neg, el)
    top2 = jnp.max(el2, axis=-1, keepdims=True)
    i2 = jnp.min(jnp.where(el2 == top2, lane, ROUTER_LANES), axis=-1, keepdims=True)
    e2 = jnp.exp(top2 - top1)
    p1 = 1.0 / (1.0 + e2)
    p2 = e2 / (1.0 + e2)
    comb = (jnp.where(lane == i1, p_group * p1, 0.0)
            + jnp.where(lane == i2, p_group * p2, 0.0))

    @pl.when((pl.program_id(0) == 0) & (i == 0))
    def _():
        cnt_ref[...] = jnp.zeros_like(cnt_ref)

    member = (lane == g_idx).astype(F32)
    before = jnp.dot(lt_ref[...], member.astype(BF16), preferred_element_type=F32) + cnt_ref[0:1, :]
    rank = jnp.sum(member * before, axis=-1, keepdims=True)
    cnt_ref[...] = cnt_ref[...] + jnp.sum(member, axis=0, keepdims=True)
    hc_ref[:, D_MODEL:] = jnp.where(lane == 0, g_idx.astype(F32), jnp.where(lane == 1, rank, comb))


def _post(x2d, attn, u, hist, lw, *, bsz, tlen, tm, alpha):
    n = x2d.shape[0]
    nt = tlen // tm
    row = lambda b, i: (b * nt + i, 0)
    const = lambda b, i: (0, 0)

    def halo_map(b, i):
        return (jnp.maximum((b * tlen + i * tm) // HALO - 1, 0), 0)

    full = lambda a: pl.BlockSpec(a.shape, lambda b, i, nd=a.ndim: (0,) * nd)
    r = lax.broadcasted_iota(jnp.int32, (tm, tm), 0)
    c = lax.broadcasted_iota(jnp.int32, (tm, tm), 1)
    earlier = jnp.where(c < r, 1.0, 0.0).astype(BF16)
    weights = [lw["wba"], lw["wbc"], lw["conv_w"], lw["conv_b"], lw["conv_ln_g"], lw["conv_ln_b"],
               lw["w_gate"], lw["b_gate"], lw["w_out"], lw["ln1_g"], lw["ln1_b"],
               lw["w_router"], lw["b_router"], earlier]
    return pl.pallas_call(
        functools.partial(_post_kernel, tm=tm, alpha=alpha),
        grid=(bsz, nt),
        in_specs=[pl.BlockSpec((tm, D_MODEL), row),
                  pl.BlockSpec((tm, SB_DIM), row),
                  pl.BlockSpec((tm, CONV_DIM), row),
                  pl.BlockSpec((HALO, CONV_DIM), halo_map),
                  pl.BlockSpec((HALO, CONV_DIM), lambda b, i: (b, 0))] + [full(w) for w in weights],
        out_specs=[pl.BlockSpec((tm, ROW_WIDTH), row),
                   pl.BlockSpec((SUBLANES, ROUTER_LANES), lambda b, i: (0, 0))],
        out_shape=[jax.ShapeDtypeStruct((n, ROW_WIDTH), F32),
                   jax.ShapeDtypeStruct((SUBLANES, ROUTER_LANES), F32)],
        scratch_shapes=[pltpu.VMEM((HALO + tm, CONV_DIM), F32),
                        pltpu.VMEM((SUBLANES - 1, HALO + tm - SUBLANES, CONV_DIM), F32),
                        pltpu.VMEM((tm, CONV_DIM), F32),
                        pltpu.VMEM((tm, D_MODEL), BF16),
                        pltpu.VMEM((tm, 2 * D_MODEL), F32),
                        pltpu.VMEM((tm, D_MODEL), F32)],
        compiler_params=pltpu.CompilerParams(dimension_semantics=("arbitrary", "arbitrary"),
                                             vmem_limit_bytes=VMEM_LIMIT),
        name="post",
    )(x2d, attn, u, u, hist, *weights)


def _combine_weight(meta, lane_id):
    lane = lax.broadcasted_iota(jnp.int32, meta.shape, 1)
    return jnp.sum(jnp.where(lane == lane_id, meta, 0.0), axis=-1, keepdims=True)


def _moe_dense_kernel(hc_ref, wg_ref, wu_ref, wd_ref, l2g_ref, l2b_ref, y_ref,
                      hb_ref, acc_ref, *, alpha):
    e = pl.program_id(1)

    @pl.when(e == 0)
    def _():
        hb_ref[...] = hc_ref[:, :D_MODEL].astype(BF16)
        acc_ref[...] = jnp.zeros_like(acc_ref)

    hb = hb_ref[...]
    gate = jnp.dot(hb, wg_ref[0], preferred_element_type=F32)
    up = jnp.dot(hb, wu_ref[0], preferred_element_type=F32)
    c_e = _combine_weight(hc_ref[:, D_MODEL:], N_GROUPS + e)
    hid = gate * jax.nn.sigmoid(gate) * up * c_e
    acc_ref[...] += jnp.dot(hid.astype(BF16), wd_ref[0], preferred_element_type=F32)

    @pl.when(e == N_EXPERTS - 1)
    def _():
        y_ref[...] = _layer_norm(alpha * hc_ref[:, :D_MODEL] + acc_ref[...],
                                 l2g_ref[...], l2b_ref[...])


def _moe_dense(hc, lw, *, tm, alpha):
    n = hc.shape[0]
    row = lambda i, e: (i, 0)
    const = lambda i, e: (0, 0)
    expert = lambda i, e: (e, 0, 0)
    return pl.pallas_call(
        functools.partial(_moe_dense_kernel, alpha=alpha),
        grid=(n // tm, N_EXPERTS),
        in_specs=[pl.BlockSpec((tm, ROW_WIDTH), row),
                  pl.BlockSpec((1, D_MODEL, EXPERT_DFF), expert),
                  pl.BlockSpec((1, D_MODEL, EXPERT_DFF), expert),
                  pl.BlockSpec((1, EXPERT_DFF, D_MODEL), expert),
                  pl.BlockSpec((1, D_MODEL), const),
                  pl.BlockSpec((1, D_MODEL), const)],
        out_specs=pl.BlockSpec((tm, D_MODEL), row),
        out_shape=jax.ShapeDtypeStruct((n, D_MODEL), F32),
        scratch_shapes=[pltpu.VMEM((tm, D_MODEL), BF16), pltpu.VMEM((tm, D_MODEL), F32)],
        compiler_params=pltpu.CompilerParams(dimension_semantics=("arbitrary", "arbitrary"),
                                             vmem_limit_bytes=VMEM_LIMIT),
        name="moe_dense",
    )(hc, lw["w_expert_gate"], lw["w_expert_up"], lw["w_expert_down"], lw["ln2_g"], lw["ln2_b"])


def _row_copy(src_ref, src_row, dst_ref, dst_row, sem):
    return pltpu.make_async_copy(src_ref.at[pl.ds(src_row, 1)], dst_ref.at[pl.ds(dst_row, 1)], sem)


def _for_each_row(rows, fn):
    def step(t, _):
        base = pl.multiple_of(t * SUBLANES, SUBLANES)
        for k in range(SUBLANES):
            fn(base + k)
        return 0

    lax.fori_loop(0, rows // SUBLANES, step, 0)


def _scatter_kernel(pad_start_ref, pad_len_ref, hc_ref, dst_ref, hs_ref, zero_ref, sem, *, rows):
    @pl.when(pl.program_id(0) == 0)
    def _():
        zero_ref[...] = jnp.zeros_like(zero_ref)
        for g in range(N_GROUPS):
            def fill(k, _):
                _row_copy(zero_ref, 0, hs_ref, pad_start_ref[g] + k, sem).start()
                return 0

            def drain(k, _):
                _row_copy(zero_ref, 0, hs_ref, pad_start_ref[g] + k, sem).wait()
                return 0

            lax.fori_loop(0, pad_len_ref[g], fill, 0)
            lax.fori_loop(0, pad_len_ref[g], drain, 0)

    _for_each_row(rows, lambda r: _row_copy(hc_ref, r, hs_ref, dst_ref[0, 0, r], sem).start())
    _for_each_row(rows, lambda r: _row_copy(hc_ref, r, hs_ref, dst_ref[0, 0, r], sem).wait())


def _gather_kernel(ys_ref, dst_ref, y_ref, sem, *, rows):
    _for_each_row(rows, lambda r: _row_copy(ys_ref, dst_ref[0, 0, r], y_ref, r, sem).start())
    _for_each_row(rows, lambda r: _row_copy(ys_ref, dst_ref[0, 0, r], y_ref, r, sem).wait())


def _grouped_kernel(group_ref, used_ref, hs_ref, wg_ref, wu_ref, wd_ref, l2g_ref, l2b_ref, ys_ref,
                    *, alpha):
    t = pl.program_id(0)

    @pl.when(t < used_ref[0])
    def _():
        h = hs_ref[:, :D_MODEL]
        meta = hs_ref[:, D_MODEL:]
        hb = h.astype(BF16)
        gate = jnp.dot(hb, wg_ref[0], preferred_element_type=F32)
        up = jnp.dot(hb, wu_ref[0], preferred_element_type=F32)
        hid = gate * jax.nn.sigmoid(gate) * up
        first = N_GROUPS + EXPERTS_PER_GROUP * group_ref[t]
        scaled = [hid[:, e * EXPERT_DFF:(e + 1) * EXPERT_DFF] * _combine_weight(meta, first + e)
                  for e in range(EXPERTS_PER_GROUP)]
        ff = jnp.dot(jnp.concatenate(scaled, axis=1).astype(BF16), wd_ref[0],
                     preferred_element_type=F32)
        ys_ref[...] = _layer_norm(alpha * h + ff, l2g_ref[...], l2b_ref[...])

    @pl.when(t >= used_ref[0])
    def _():
        ys_ref[...] = jnp.zeros_like(ys_ref)


def _moe_sparse(hc, counts, lw, *, alpha):
    n = hc.shape[0]
    n_tiles = n // MOE_TILE + N_GROUPS
    n_sorted = n_tiles * MOE_TILE
    counts = counts.astype(jnp.int32)
    tiles = (counts + MOE_TILE - 1) // MOE_TILE
    tile_end = jnp.cumsum(tiles)
    first_row = (tile_end - tiles) * MOE_TILE
    group = hc[:, D_MODEL].astype(jnp.int32)
    rank = hc[:, D_MODEL + 1].astype(jnp.int32)
    dst = (first_row[group] + rank).reshape(n // MOVE_TILE, 1, MOVE_TILE)
    tile_group = jnp.minimum(
        jnp.sum(jnp.arange(n_tiles, dtype=jnp.int32)[:, None] >= tile_end[None, :], axis=1),
        N_GROUPS - 1).astype(jnp.int32)
    used = tile_end[N_GROUPS - 1:].astype(jnp.int32)
    pad_start = (first_row + counts).astype(jnp.int32)
    next_start = jnp.concatenate([first_row[1:], jnp.full((1,), n_sorted, jnp.int32)])
    pad_len = (next_start - pad_start).astype(jnp.int32)

    moved = pltpu.CompilerParams(dimension_semantics=("arbitrary",), vmem_limit_bytes=VMEM_LIMIT)
    hs = pl.pallas_call(
        functools.partial(_scatter_kernel, rows=MOVE_TILE),
        grid_spec=pltpu.PrefetchScalarGridSpec(
            num_scalar_prefetch=2,
            grid=(n // MOVE_TILE,),
            in_specs=[pl.BlockSpec((MOVE_TILE, ROW_WIDTH), lambda i, ps, pn: (i, 0)),
                      pl.BlockSpec((1, 1, MOVE_TILE), lambda i, ps, pn: (i, 0, 0),
                                   memory_space=pltpu.SMEM)],
            out_specs=pl.BlockSpec(memory_space=pl.ANY),
            scratch_shapes=[pltpu.VMEM((SUBLANES, ROW_WIDTH), F32), pltpu.SemaphoreType.DMA(())]),
        out_shape=jax.ShapeDtypeStruct((n_sorted, ROW_WIDTH), F32),
        compiler_params=moved,
        name="moe_scatter",
    )(pad_start, pad_len, hc, dst)

    group_w = lambda t, grp, use: (grp[t], 0, 0)
    ys = pl.pallas_call(
        functools.partial(_grouped_kernel, alpha=alpha),
        grid_spec=pltpu.PrefetchScalarGridSpec(
            num_scalar_prefetch=2,
            grid=(n_tiles,),
            in_specs=[pl.BlockSpec((MOE_TILE, ROW_WIDTH), lambda t, grp, use: (t, 0)),
                      pl.BlockSpec((1, D_MODEL, D_MODEL), group_w),
                      pl.BlockSpec((1, D_MODEL, D_MODEL), group_w),
                      pl.BlockSpec((1, D_MODEL, D_MODEL), group_w),
                      pl.BlockSpec((1, D_MODEL), lambda t, grp, use: (0, 0)),
                      pl.BlockSpec((1, D_MODEL), lambda t, grp, use: (0, 0))],
            out_specs=pl.BlockSpec((MOE_TILE, D_MODEL), lambda t, grp, use: (t, 0))),
        out_shape=jax.ShapeDtypeStruct((n_sorted, D_MODEL), F32),
        compiler_params=moved,
        name="moe_grouped",
    )(tile_group, used, hs, lw["w_group_gate"], lw["w_group_up"], lw["w_group_down"],
      lw["ln2_g"], lw["ln2_b"])

    return pl.pallas_call(
        functools.partial(_gather_kernel, rows=MOVE_TILE),
        grid=(n // MOVE_TILE,),
        in_specs=[pl.BlockSpec(memory_space=pl.ANY),
                  pl.BlockSpec((1, 1, MOVE_TILE), lambda i: (i, 0, 0), memory_space=pltpu.SMEM)],
        out_specs=pl.BlockSpec((MOVE_TILE, D_MODEL), lambda i: (i, 0)),
        out_shape=jax.ShapeDtypeStruct((n, D_MODEL), F32),
        scratch_shapes=[pltpu.SemaphoreType.DMA(())],
        compiler_params=moved,
        name="moe_gather",
    )(ys, dst)


def _encoder_layer(x, k_hist, v_hist, conv_hist, lw, *, alpha, proj_tm, post_tm, moe_tm, tq, pairs):
    bsz, tlen, _ = x.shape
    n = bsz * tlen
    x2d = x.reshape(n, D_MODEL)
    q, k, v, kb, vb, u = _proj(x2d, lw["w_in"], proj_tm)

    nq = tlen // tq
    if k_hist is None:
        kn, vn, new_rows, ck, cv = kb, vb, tlen, None, None
    else:
        pad = lambda a: jnp.pad(a.reshape(bsz, tlen, SB_DIM),
                                ((0, 0), (0, KEY_TILE - tlen), (0, 0))).reshape(bsz * KEY_TILE, SB_DIM)
        kn, vn, new_rows, ck, cv = pad(kb), pad(vb), KEY_TILE, k_hist, v_hist
    attn = _attention(q, kn, vn, ck, cv, bsz=bsz, tq=tq, nq=nq, new_rows=new_rows, pairs=pairs)

    if conv_hist is None:
        hist = jnp.zeros((bsz * HALO, CONV_DIM), F32)
    else:
        hist = jnp.pad(conv_hist, ((0, 0), (HALO - (CONV_K - 1), 0), (0, 0))).reshape(bsz * HALO, CONV_DIM)
    hc, counts = _post(x2d, attn, u, hist, lw, bsz=bsz, tlen=tlen, tm=post_tm, alpha=alpha)
    if n % MOVE_TILE == 0:
        y = _moe_sparse(hc, counts[0, :N_GROUPS], lw, alpha=alpha)
    else:
        y = _moe_dense(hc, lw, tm=moe_tm, alpha=alpha)

    u3 = u.reshape(bsz, tlen, CONV_DIM)
    if conv_hist is None:
        conv_state = u3[:, tlen - (CONV_K - 1):]
    else:
        conv_state = jnp.concatenate([conv_hist, u3], axis=1)[:, -(CONV_K - 1):]
    return (y.reshape(bsz, tlen, D_MODEL),
            k.reshape(bsz, tlen, SB_HEADS, SB_HEAD_DIM),
            v.reshape(bsz, tlen, SB_HEADS, SB_HEAD_DIM),
            conv_state)


def _by_group_cols(w):
    w = w.astype(BF16).reshape(N_GROUPS, EXPERTS_PER_GROUP, D_MODEL, EXPERT_DFF)
    return w.transpose(0, 2, 1, 3).reshape(N_GROUPS, D_MODEL, EXPERTS_PER_GROUP * EXPERT_DFF)


def _layer_weights(l, w_in, w_branch_attn, w_branch_conv, conv_w, conv_b, conv_ln_g, conv_ln_b,
                   w_gate, b_gate, w_out, ln1_g, ln1_b, w_router_group, b_router_group,
                   w_router_expert, b_router_expert, w_expert_gate, w_expert_up, w_expert_down,
                   ln2_g, ln2_b):
    n_route = N_GROUPS + N_EXPERTS
    w_router = jnp.concatenate([w_router_group[l], w_router_expert[l]], axis=1)
    w_router = jnp.pad(w_router, ((0, 0), (0, ROUTER_LANES - n_route)))
    b_router = jnp.pad(jnp.concatenate([b_router_group[l], b_router_expert[l]]),
                       (0, ROUTER_LANES - n_route))
    row = lambda a: a.reshape(1, -1).astype(F32)
    return dict(
        w_in=w_in[l].astype(BF16), wba=w_branch_attn[l].astype(BF16),
        wbc=w_branch_conv[l].astype(BF16),
        conv_w=jnp.broadcast_to(conv_w[l].astype(F32)[:, None, :], (CONV_K, SUBLANES, CONV_DIM)),
        conv_b=row(conv_b[l]),
        conv_ln_g=row(conv_ln_g[l]), conv_ln_b=row(conv_ln_b[l]),
        w_gate=w_gate[l].astype(BF16), b_gate=row(b_gate[l]), w_out=w_out[l].astype(BF16),
        ln1_g=row(ln1_g[l]), ln1_b=row(ln1_b[l]),
        w_router=w_router.astype(BF16), b_router=row(b_router),
        w_expert_gate=w_expert_gate[l].astype(BF16), w_expert_up=w_expert_up[l].astype(BF16),
        w_expert_down=w_expert_down[l].astype(BF16), ln2_g=row(ln2_g[l]), ln2_b=row(ln2_b[l]),
        w_group_gate=_by_group_cols(w_expert_gate[l]), w_group_up=_by_group_cols(w_expert_up[l]),
        w_group_down=w_expert_down[l].astype(BF16).reshape(
            N_GROUPS, EXPERTS_PER_GROUP * EXPERT_DFF, D_MODEL))


def kernel(x_prompt, x_sample, cache_k, cache_v, state_conv, w_in, w_branch_attn, w_branch_conv, conv_w, conv_b, conv_ln_g, conv_ln_b, w_gate, b_gate, w_out, ln1_g, ln1_b, w_router_group, b_router_group, w_router_expert, b_router_expert, w_expert_gate, w_expert_up, w_expert_down, ln2_g, ln2_b):
    depth = w_in.shape[0]
    alpha = (2.0 * depth) ** 0.25
    h_p, h_s = x_prompt, x_sample
    outs = [[] for _ in range(6)]
    for l in range(depth):
        lw = _layer_weights(l, w_in, w_branch_attn, w_branch_conv, conv_w, conv_b, conv_ln_g,
                            conv_ln_b, w_gate, b_gate, w_out, ln1_g, ln1_b, w_router_group,
                            b_router_group, w_router_expert, b_router_expert, w_expert_gate,
                            w_expert_up, w_expert_down, ln2_g, ln2_b)
        n_p = h_p.shape[0] * h_p.shape[1]
        h_p, k_p, v_p, c_p = _encoder_layer(
            h_p, None, None, None, lw, alpha=alpha,
            proj_tm=min(512, n_p), post_tm=min(512, h_p.shape[1]), moe_tm=min(1024, n_p),
            tq=128, pairs=4)
        ts = h_s.shape[1]
        h_s, k_s, v_s, c_s = _encoder_layer(
            h_s, cache_k[l], cache_v[l], state_conv[l], lw, alpha=alpha,
            proj_tm=h_s.shape[0] * ts, post_tm=ts, moe_tm=h_s.shape[0] * ts, tq=ts, pairs=4)
        for lst, val in zip(outs, (k_p, v_p, c_p, k_s, v_s, c_s)):
            lst.append(val)
    return (h_p, h_s) + tuple(jnp.stack(o) for o in outs)
```

```python
import functools

import jax
import jax.numpy as jnp
from jax import lax
from jax.experimental import pallas as pl
from jax.experimental.pallas import tpu as pltpu

F32 = jnp.float32
BF16 = jnp.bfloat16

D_MODEL = 1024
SB_DIM = 512
SB_HEADS = 8
SB_HEAD_DIM = 64
SB_SCALE = SB_HEAD_DIM ** -0.5
CONV_DIM = 512
CONV_K = 31
N_GROUPS = 4
EXPERTS_PER_GROUP = 4
N_EXPERTS = 16
EXPERT_DFF = 256
LN_EPS = 1e-5
IN_DIM = 3 * SB_DIM + 2 * CONV_DIM

LANES = 128
SUBLANES = 8
CONV_CHUNK = 32
MXU_COLS = 256
KEY_TILE = 128
CACHE_WINDOW_TILES = 4
HALO = 32
ROUTER_LANES = 128
ROW_WIDTH = D_MODEL + ROUTER_LANES
MOE_TILE = 256
MOVE_TILE = 2048
VMEM_LIMIT = 56 * 1024 * 1024
EXP_UNDERFLOW = -88.0
MASKED_LOG_WEIGHT = -1e30


def _layer_norm(x, g, b):
    mu = jnp.mean(x, axis=-1, keepdims=True)
    xc = x - mu
    var = jnp.mean(xc * xc, axis=-1, keepdims=True)
    return xc * lax.rsqrt(var + LN_EPS) * g + b


def _proj_kernel(x_ref, w_ref, wkv_t_ref, q_ref, kt_ref, vt_ref, ktb_ref, vtb_ref, u_ref):
    xb = x_ref[...].astype(BF16)

    def mm(lo, hi):
        return jnp.dot(xb, w_ref[:, lo:hi], preferred_element_type=F32)

    q_ref[...] = (mm(0, SB_DIM) * SB_SCALE).astype(BF16)
    kv_t = lax.dot_general(wkv_t_ref[...], xb, (((1,), (1,)), ((), ())), preferred_element_type=F32)
    kt_ref[0] = kv_t[:SB_DIM]
    ktb_ref[0] = kv_t[:SB_DIM].astype(BF16)
    vt_ref[0] = kv_t[SB_DIM:]
    vtb_ref[0] = kv_t[SB_DIM:].astype(BF16)
    val = mm(3 * SB_DIM, 3 * SB_DIM + CONV_DIM)
    gate = mm(3 * SB_DIM + CONV_DIM, IN_DIM)
    u_ref[...] = val * jax.nn.sigmoid(gate)


def _proj(x2d, w_in_bf, wkv_t_bf, *, bsz, tlen, tm):
    n = x2d.shape[0]
    nt = tlen // tm
    row = lambda i: (i, 0)
    feat = lambda i: (i // nt, 0, i % nt)
    outs = [jax.ShapeDtypeStruct((n, SB_DIM), BF16),
            jax.ShapeDtypeStruct((bsz, SB_DIM, tlen), F32),
            jax.ShapeDtypeStruct((bsz, SB_DIM, tlen), F32),
            jax.ShapeDtypeStruct((bsz, SB_DIM, tlen), BF16),
            jax.ShapeDtypeStruct((bsz, SB_DIM, tlen), BF16),
            jax.ShapeDtypeStruct((n, CONV_DIM), F32)]
    return pl.pallas_call(
        _proj_kernel,
        grid=(n // tm,),
        in_specs=[pl.BlockSpec((tm, D_MODEL), row),
                  pl.BlockSpec((D_MODEL, IN_DIM), lambda i: (0, 0)),
                  pl.BlockSpec((2 * SB_DIM, D_MODEL), lambda i: (0, 0))],
        out_specs=([pl.BlockSpec((tm, SB_DIM), row)] + [pl.BlockSpec((1, SB_DIM, tm), feat)] * 4
                   + [pl.BlockSpec((tm, CONV_DIM), row)]),
        out_shape=outs,
        compiler_params=pltpu.CompilerParams(dimension_semantics=("arbitrary",),
                                             vmem_limit_bytes=VMEM_LIMIT),
        name="proj",
    )(x2d, w_in_bf, wkv_t_bf)


def _attn_kernel(*refs, tq, pairs, n_cache_tiles, window_tiles):
    if n_cache_tiles:
        (q_ref, kn_ref, vn_ref, kwin_ref, vwin_ref, kc_hbm, vc_hbm, uu_ref, o_ref,
         qs_ref, acc_ref, run_ref, zs_ref, vs_ref, kst_ref, vst_ref, kdeep_ref, vdeep_ref,
         sem) = refs
    else:
        q_ref, kn_ref, vn_ref, uu_ref, o_ref, qs_ref, acc_ref, run_ref, zs_ref, vs_ref = refs
    i = pl.program_id(1)
    rows = 2 * tq
    uu = uu_ref[...]

    lane_q = lax.broadcasted_iota(jnp.int32, (tq, LANES), 1)
    feat_k = lax.broadcasted_iota(jnp.int32, (LANES, KEY_TILE), 0)
    r_idx = lax.broadcasted_iota(jnp.int32, (rows, KEY_TILE), 0)
    r_idx = jnp.where(r_idx >= tq, r_idx - tq, r_idx)
    c_idx = lax.broadcasted_iota(jnp.int32, (rows, KEY_TILE), 1)
    causal = c_idx < r_idx

    def score_sweep(slot, load_kv, mask, first):
        zs, splits = [], []
        for p in range(pairs):
            kt, vt = load_kv(slice(p * LANES, (p + 1) * LANES))
            vs_ref[slot, p, 0] = jnp.where(feat_k < SB_HEAD_DIM, vt, jnp.zeros_like(vt))
            vs_ref[slot, p, 1] = jnp.where(feat_k >= SB_HEAD_DIM, vt, jnp.zeros_like(vt))
            z = jnp.dot(qs_ref[p], kt, preferred_element_type=F32)
            lk = -(jnp.maximum(z, 0.0) + jnp.log(1.0 + jnp.exp(-jnp.abs(z))))
            if mask is not None:
                lk = jnp.where(mask, lk, 0.0)
            hi = lk.astype(BF16)
            lo = (lk - hi.astype(F32)).astype(BF16)
            zs.append(z)
            splits.append(jnp.concatenate([hi, lo], axis=1))
        su = jnp.dot(jnp.concatenate(splits, axis=0), uu, preferred_element_type=F32)
        worst = None
        for p in range(pairs):
            sp = su[p * rows:(p + 1) * rows]
            logw = zs[p] + sp[:, :KEY_TILE]
            tot = sp[:, KEY_TILE:]
            if not first:
                run = run_ref[p]
                logw = logw + run
                tot = tot + run
            if mask is not None:
                logw = jnp.where(mask, logw, MASKED_LOG_WEIGHT)
            zs_ref[slot, p] = logw
            run_ref[p] = tot
            worst = tot if worst is None else jnp.maximum(worst, tot)
        return jnp.max(worst)

    def value_sweep(slot):
        over_keys = (((1,), (1,)), ((), ()))
        for p in range(pairs):
            wb = jnp.exp(zs_ref[slot, p]).astype(BF16)
            acc_ref[p] += (
                lax.dot_general(wb[:tq], vs_ref[slot, p, 0], over_keys, preferred_element_type=F32)
                + lax.dot_general(wb[tq:], vs_ref[slot, p, 1], over_keys, preferred_element_type=F32))

    for p in range(pairs):
        qp = q_ref[:, p * LANES:(p + 1) * LANES]
        qs_ref[p] = jnp.concatenate(
            [jnp.where(lane_q < SB_HEAD_DIM, qp, jnp.zeros_like(qp)),
             jnp.where(lane_q >= SB_HEAD_DIM, qp, jnp.zeros_like(qp))], axis=0)

    def new_tile(j):
        keys = pl.ds(pl.multiple_of(j * KEY_TILE, KEY_TILE), KEY_TILE)
        return lambda feats: (kn_ref[0, feats, keys], vn_ref[0, feats, keys])

    def cache_tile(j):
        first_window_tile = n_cache_tiles - window_tiles

        @pl.when(j >= first_window_tile)
        def _():
            keys = pl.ds(pl.multiple_of((j - first_window_tile) * KEY_TILE, KEY_TILE), KEY_TILE)
            kst_ref[...] = kwin_ref[0, :, keys].astype(BF16)
            vst_ref[...] = vwin_ref[0, :, keys].astype(BF16)

        @pl.when(j < first_window_tile)
        def _():
            b = pl.program_id(0)
            keys = pl.ds(pl.multiple_of(j * KEY_TILE, KEY_TILE), KEY_TILE)
            fetch_k = pltpu.make_async_copy(kc_hbm.at[b, :, keys], kdeep_ref, sem.at[0])
            fetch_v = pltpu.make_async_copy(vc_hbm.at[b, :, keys], vdeep_ref, sem.at[1])
            fetch_k.start()
            fetch_v.start()
            fetch_k.wait()
            fetch_v.wait()
            kst_ref[...] = kdeep_ref[...].astype(BF16)
            vst_ref[...] = vdeep_ref[...].astype(BF16)

        return lambda feats: (kst_ref[feats, :], vst_ref[feats, :])

    live = lambda c: (c[0] >= 0) & (c[2] > EXP_UNDERFLOW)

    def pipelined(tile_of):
        def body(c):
            j, slot, _ = c
            value_sweep(slot)
            return j - 1, 1 - slot, score_sweep(1 - slot, tile_of(j), None, False)
        return body

    worst = score_sweep(0, new_tile(i), causal, True)
    acc_ref[...] = jnp.zeros_like(acc_ref)
    carry = lax.while_loop(live, pipelined(new_tile), (i - 1, jnp.int32(0), worst))
    if n_cache_tiles:
        carry = lax.while_loop(live, pipelined(cache_tile),
                               (jnp.int32(n_cache_tiles - 1), carry[1], carry[2]))
    value_sweep(carry[1])

    for p in range(pairs):
        o_ref[:, p * LANES:(p + 1) * LANES] = acc_ref[p].astype(BF16)


def _suffix_sum_matrix():
    r = lax.broadcasted_iota(jnp.int32, (2 * KEY_TILE, 2 * KEY_TILE), 0) % KEY_TILE
    c = lax.broadcasted_iota(jnp.int32, (2 * KEY_TILE, 2 * KEY_TILE), 1)
    return jnp.where((c >= KEY_TILE) | (r >= c), 1.0, 0.0).astype(BF16)


def _attention(q, kn, vn, cache_k, cache_v, *, bsz, tq, nq):
    n = q.shape[0]
    pairs = SB_DIM // LANES
    new_cols = kn.shape[2]
    n_cache_tiles = 0 if cache_k is None else cache_k.shape[2] // KEY_TILE
    window_tiles = 0
    in_specs = [pl.BlockSpec((tq, SB_DIM), lambda b, i: (b * nq + i, 0)),
                pl.BlockSpec((1, SB_DIM, new_cols), lambda b, i: (b, 0, 0)),
                pl.BlockSpec((1, SB_DIM, new_cols), lambda b, i: (b, 0, 0))]
    args = [q, kn, vn]
    scratch = [pltpu.VMEM((pairs, 2 * tq, LANES), BF16),
               pltpu.VMEM((pairs, tq, LANES), F32),
               pltpu.VMEM((pairs, 2 * tq, KEY_TILE), F32),
               pltpu.VMEM((2, pairs, 2 * tq, KEY_TILE), F32),
               pltpu.VMEM((2, pairs, 2, LANES, KEY_TILE), BF16)]
    if n_cache_tiles:
        assert cache_k.shape[2] % KEY_TILE == 0
        window_tiles = CACHE_WINDOW_TILES if n_cache_tiles % CACHE_WINDOW_TILES == 0 else 1
        window = (1, SB_DIM, window_tiles * KEY_TILE)
        last = n_cache_tiles // window_tiles - 1
        in_specs += [pl.BlockSpec(window, lambda b, i: (b, 0, last))] * 2
        in_specs += [pl.BlockSpec(memory_space=pl.ANY)] * 2
        args += [cache_k, cache_v, cache_k, cache_v]
        scratch += [pltpu.VMEM((SB_DIM, KEY_TILE), BF16),
                    pltpu.VMEM((SB_DIM, KEY_TILE), BF16),
                    pltpu.VMEM((SB_DIM, KEY_TILE), F32),
                    pltpu.VMEM((SB_DIM, KEY_TILE), F32),
                    pltpu.SemaphoreType.DMA((2,))]
    in_specs.append(pl.BlockSpec((2 * KEY_TILE, 2 * KEY_TILE), lambda b, i: (0, 0)))
    args.append(_suffix_sum_matrix())
    return pl.pallas_call(
        functools.partial(_attn_kernel, tq=tq, pairs=pairs, n_cache_tiles=n_cache_tiles,
                          window_tiles=window_tiles),
        grid=(bsz, nq),
        in_specs=in_specs,
        out_specs=pl.BlockSpec((tq, SB_DIM), lambda b, i: (b * nq + i, 0)),
        out_shape=jax.ShapeDtypeStruct((n, SB_DIM), BF16),
        scratch_shapes=scratch,
        compiler_params=pltpu.CompilerParams(
            dimension_semantics=("arbitrary", "arbitrary"),
            vmem_limit_bytes=VMEM_LIMIT),
        name="attn",
    )(*args)


def _post_kernel(x_ref, attn_ref, u_ref, halo_ref, hist_ref, wba_ref, wbc_ref, cw_ref, cb_ref,
                 clg_ref, clb_ref, wg_ref, bg_ref, wo_ref, l1g_ref, l1b_ref, wr_ref, br_ref,
                 lt_ref, hc_ref, cnt_ref, ext_ref, sh_ref, conv_ref, xb_ref, gates_ref, ba_ref,
                 *, tm, alpha):
    i = pl.program_id(1)
    prev = jnp.where(i == 0, hist_ref[...], halo_ref[...])
    ext_ref[0:HALO, :] = prev
    ext_ref[HALO:HALO + tm, :] = u_ref[...]
    span = tm + HALO - SUBLANES
    for s in range(1, SUBLANES):
        sh_ref[s - 1] = ext_ref[s:s + span, :]
    xb_ref[...] = x_ref[...].astype(BF16)

    def gate_block(n):
        cols = slice(n * MXU_COLS, (n + 1) * MXU_COLS)
        gates_ref[:, cols] = jax.nn.sigmoid(
            jnp.dot(xb_ref[...], wg_ref[:, cols], preferred_element_type=F32) + bg_ref[:, cols])

    def attn_block(n):
        cols = slice(n * MXU_COLS, (n + 1) * MXU_COLS)
        ba_ref[:, cols] = jnp.dot(attn_ref[...], wba_ref[:, cols], preferred_element_type=F32)

    jobs = ([functools.partial(gate_block, n) for n in range(2 * D_MODEL // MXU_COLS)]
            + [functools.partial(attn_block, n) for n in range(D_MODEL // MXU_COLS)])

    first = HALO - (CONV_K - 1)
    chunk = min(CONV_CHUNK, tm)
    tiles = chunk // SUBLANES
    for idx, r0 in enumerate(range(0, tm, chunk)):
        acc = jnp.zeros((tiles, SUBLANES, CONV_DIM), F32) + cb_ref[...]
        for j in range(CONV_K):
            phase = (first + j) % SUBLANES
            base = first + j - phase + r0
            rows = ext_ref[base:base + chunk, :] if phase == 0 else sh_ref[phase - 1, base:base + chunk, :]
            acc = acc + rows.reshape(tiles, SUBLANES, CONV_DIM) * cw_ref[j]
        conv_ref[r0:r0 + chunk, :] = acc.reshape(chunk, CONV_DIM)
        if jobs:
            jobs.pop(0)()
    while jobs:
        jobs.pop(0)()

    c = _layer_norm(conv_ref[...], clg_ref[...], clb_ref[...])
    c = c * jax.nn.sigmoid(c)
    branch_b = jnp.dot(c.astype(BF16), wbc_ref[...], preferred_element_type=F32)
    x = x_ref[...]
    m = gates_ref[:, :D_MODEL] * ba_ref[...] + gates_ref[:, D_MODEL:] * branch_b
    mixed = jnp.dot(m.astype(BF16), wo_ref[...], preferred_element_type=F32)
    h = _layer_norm(alpha * x + mixed, l1g_ref[...], l1b_ref[...])
    hc_ref[:, :D_MODEL] = h

    logits = jnp.dot(h.astype(BF16), wr_ref[...], preferred_element_type=F32) + br_ref[...]
    lane = lax.broadcasted_iota(jnp.int32, (tm, ROUTER_LANES), 1)
    neg = jnp.float32(-jnp.inf)
    gl = jnp.where(lane < N_GROUPS, logits, neg)
    gmax = jnp.max(gl, axis=-1, keepdims=True)
    g_idx = jnp.min(jnp.where(gl == gmax, lane, ROUTER_LANES), axis=-1, keepdims=True)
    p_group = 1.0 / jnp.sum(jnp.exp(gl - gmax), axis=-1, keepdims=True)
    lo = N_GROUPS + EXPERTS_PER_GROUP * g_idx
    el = jnp.where((lane >= lo) & (lane < lo + EXPERTS_PER_GROUP), logits, neg)
    top1 = jnp.max(el, axis=-1, keepdims=True)
    i1 = jnp.min(jnp.where(el == top1, lane, ROUTER_LANES), axis=-1, keepdims=True)
    el2 = jnp.where(lane == i1, neg, el)
    top2 = jnp.max(el2, axis=-1, keepdims=True)
    i2 = jnp.min(jnp.where(el2 == top2, lane, ROUTER_LANES), axis=-1, keepdims=True)
    e2 = jnp.exp(top2 - top1)
    p1 = 1.0 / (1.0 + e2)
    p2 = e2 / (1.0 + e2)
    comb = (jnp.where(lane == i1, p_group * p1, 0.0)
            + jnp.where(lane == i2, p_group * p2, 0.0))

    @pl.when((pl.program_id(0) == 0) & (i == 0))
    def _():
        cnt_ref[...] = jnp.zeros_like(cnt_ref)

    member = (lane == g_idx).astype(F32)
    before = jnp.dot(lt_ref[...], member.astype(BF16), preferred_element_type=F32) + cnt_ref[0:1, :]
    rank = jnp.sum(member * before, axis=-1, keepdims=True)
    cnt_ref[...] = cnt_ref[...] + jnp.sum(member, axis=0, keepdims=True)
    hc_ref[:, D_MODEL:] = jnp.where(lane == 0, g_idx.astype(F32), jnp.where(lane == 1, rank, comb))


def _post(x2d, attn, u, hist, lw, *, bsz, tlen, tm, alpha):
    n = x2d.shape[0]
    nt = tlen // tm
    row = lambda b, i: (b * nt + i, 0)
    const = lambda b, i: (0, 0)

    def halo_map(b, i):
        return (jnp.maximum((b * tlen + i * tm) // HALO - 1, 0), 0)

    full = lambda a: pl.BlockSpec(a.shape, lambda b, i, nd=a.ndim: (0,) * nd)
    r = lax.broadcasted_iota(jnp.int32, (tm, tm), 0)
    c = lax.broadcasted_iota(jnp.int32, (tm, tm), 1)
    earlier = jnp.where(c < r, 1.0, 0.0).astype(BF16)
    weights = [lw["wba"], lw["wbc"], lw["conv_w"], lw["conv_b"], lw["conv_ln_g"], lw["conv_ln_b"],
               lw["w_gate"], lw["b_gate"], lw["w_out"], lw["ln1_g"], lw["ln1_b"],
               lw["w_router"], lw["b_router"], earlier]
    return pl.pallas_call(
        functools.partial(_post_kernel, tm=tm, alpha=alpha),
        grid=(bsz, nt),
        in_specs=[pl.BlockSpec((tm, D_MODEL), row),
                  pl.BlockSpec((tm, SB_DIM), row),
                  pl.BlockSpec((tm, CONV_DIM), row),
                  pl.BlockSpec((HALO, CONV_DIM), halo_map),
                  pl.BlockSpec((HALO, CONV_DIM), lambda b, i: (b, 0))] + [full(w) for w in weights],
        out_specs=[pl.BlockSpec((tm, ROW_WIDTH), row),
                   pl.BlockSpec((SUBLANES, ROUTER_LANES), lambda b, i: (0, 0))],
        out_shape=[jax.ShapeDtypeStruct((n, ROW_WIDTH), F32),
                   jax.ShapeDtypeStruct((SUBLANES, ROUTER_LANES), F32)],
        scratch_shapes=[pltpu.VMEM((HALO + tm, CONV_DIM), F32),
                        pltpu.VMEM((SUBLANES - 1, HALO + tm - SUBLANES, CONV_DIM), F32),
                        pltpu.VMEM((tm, CONV_DIM), F32),
                        pltpu.VMEM((tm, D_MODEL), BF16),
                        pltpu.VMEM((tm, 2 * D_MODEL), F32),
                        pltpu.VMEM((tm, D_MODEL), F32)],
        compiler_params=pltpu.CompilerParams(dimension_semantics=("arbitrary", "arbitrary"),
                                             vmem_limit_bytes=VMEM_LIMIT),
        name="post",
    )(x2d, attn, u, u, hist, *weights)


def _combine_weight(meta, lane_id):
    lane = lax.broadcasted_iota(jnp.int32, meta.shape, 1)
    return jnp.sum(jnp.where(lane == lane_id, meta, 0.0), axis=-1, keepdims=True)


def _moe_dense_kernel(hc_ref, wg_ref, wu_ref, wd_ref, l2g_ref, l2b_ref, y_ref,
                      hb_ref, acc_ref, *, alpha):
    e = pl.program_id(1)

    @pl.when(e == 0)
    def _():
        hb_ref[...] = hc_ref[:, :D_MODEL].astype(BF16)
        acc_ref[...] = jnp.zeros_like(acc_ref)

    hb = hb_ref[...]
    gate = jnp.dot(hb, wg_ref[0], preferred_element_type=F32)
    up = jnp.dot(hb, wu_ref[0], preferred_element_type=F32)
    c_e = _combine_weight(hc_ref[:, D_MODEL:], N_GROUPS + e)
    hid = gate * jax.nn.sigmoid(gate) * up * c_e
    acc_ref[...] += jnp.dot(hid.astype(BF16), wd_ref[0], preferred_element_type=F32)

    @pl.when(e == N_EXPERTS - 1)
    def _():
        y_ref[...] = _layer_norm(alpha * hc_ref[:, :D_MODEL] + acc_ref[...],
                                 l2g_ref[...], l2b_ref[...])


def _moe_dense(hc, lw, *, tm, alpha):
    n = hc.shape[0]
    row = lambda i, e: (i, 0)
    const = lambda i, e: (0, 0)
    expert = lambda i, e: (e, 0, 0)
    return pl.pallas_call(
        functools.partial(_moe_dense_kernel, alpha=alpha),
        grid=(n // tm, N_EXPERTS),
        in_specs=[pl.BlockSpec((tm, ROW_WIDTH), row),
                  pl.BlockSpec((1, D_MODEL, EXPERT_DFF), expert),
                  pl.BlockSpec((1, D_MODEL, EXPERT_DFF), expert),
                  pl.BlockSpec((1, EXPERT_DFF, D_MODEL), expert),
                  pl.BlockSpec((1, D_MODEL), const),
                  pl.BlockSpec((1, D_MODEL), const)],
        out_specs=pl.BlockSpec((tm, D_MODEL), row),
        out_shape=jax.ShapeDtypeStruct((n, D_MODEL), F32),
        scratch_shapes=[pltpu.VMEM((tm, D_MODEL), BF16), pltpu.VMEM((tm, D_MODEL), F32)],
        compiler_params=pltpu.CompilerParams(dimension_semantics=("arbitrary", "arbitrary"),
                                             vmem_limit_bytes=VMEM_LIMIT),
        name="moe_dense",
    )(hc, lw["w_expert_gate"], lw["w_expert_up"], lw["w_expert_down"], lw["ln2_g"], lw["ln2_b"])


def _row_copy(src_ref, src_row, dst_ref, dst_row, sem):
    return pltpu.make_async_copy(src_ref.at[pl.ds(src_row, 1)], dst_ref.at[pl.ds(dst_row, 1)], sem)


def _for_each_row(rows, fn):
    def step(t, _):
        base = pl.multiple_of(t * SUBLANES, SUBLANES)
        for k in range(SUBLANES):
            fn(base + k)
        return 0

    lax.fori_loop(0, rows // SUBLANES, step, 0)


def _scatter_kernel(pad_start_ref, pad_len_ref, hc_ref, dst_ref, hs_ref, zero_ref, sem, *, rows):
    @pl.when(pl.program_id(0) == 0)
    def _():
        zero_ref[...] = jnp.zeros_like(zero_ref)
        for g in range(N_GROUPS):
            def fill(k, _):
                _row_copy(zero_ref, 0, hs_ref, pad_start_ref[g] + k, sem).start()
                return 0

            def drain(k, _):
                _row_copy(zero_ref, 0, hs_ref, pad_start_ref[g] + k, sem).wait()
                return 0

            lax.fori_loop(0, pad_len_ref[g], fill, 0)
            lax.fori_loop(0, pad_len_ref[g], drain, 0)

    _for_each_row(rows, lambda r: _row_copy(hc_ref, r, hs_ref, dst_ref[0, 0, r], sem).start())
    _for_each_row(rows, lambda r: _row_copy(hc_ref, r, hs_ref, dst_ref[0, 0, r], sem).wait())


def _gather_kernel(ys_ref, dst_ref, y_ref, sem, *, rows):
    _for_each_row(rows, lambda r: _row_copy(ys_ref, dst_ref[0, 0, r], y_ref, r, sem).start())
    _for_each_row(rows, lambda r: _row_copy(ys_ref, dst_ref[0, 0, r], y_ref, r, sem).wait())


def _grouped_kernel(group_ref, used_ref, hs_ref, wg_ref, wu_ref, wd_ref, l2g_ref, l2b_ref, ys_ref,
                    *, alpha):
    t = pl.program_id(0)

    @pl.when(t < used_ref[0])
    def _():
        h = hs_ref[:, :D_MODEL]
        meta = hs_ref[:, D_MODEL:]
        hb = h.astype(BF16)
        gate = jnp.dot(hb, wg_ref[0], preferred_element_type=F32)
        up = jnp.dot(hb, wu_ref[0], preferred_element_type=F32)
        hid = gate * jax.nn.sigmoid(gate) * up
        first = N_GROUPS + EXPERTS_PER_GROUP * group_ref[t]
        scaled = [hid[:, e * EXPERT_DFF:(e + 1) * EXPERT_DFF] * _combine_weight(meta, first + e)
                  for e in range(EXPERTS_PER_GROUP)]
        ff = jnp.dot(jnp.concatenate(scaled, axis=1).astype(BF16), wd_ref[0],
                     preferred_element_type=F32)
        ys_ref[...] = _layer_norm(alpha * h + ff, l2g_ref[...], l2b_ref[...])

    @pl.when(t >= used_ref[0])
    def _():
        ys_ref[...] = jnp.zeros_like(ys_ref)


def _moe_sparse(hc, counts, lw, *, alpha):
    n = hc.shape[0]
    n_tiles = n // MOE_TILE + N_GROUPS
    n_sorted = n_tiles * MOE_TILE
    counts = counts.astype(jnp.int32)
    tiles = (counts + MOE_TILE - 1) // MOE_TILE
    tile_end = jnp.cumsum(tiles)
    first_row = (tile_end - tiles) * MOE_TILE
    group = hc[:, D_MODEL].astype(jnp.int32)
    rank = hc[:, D_MODEL + 1].astype(jnp.int32)
    dst = (first_row[group] + rank).reshape(n // MOVE_TILE, 1, MOVE_TILE)
    tile_group = jnp.minimum(
        jnp.sum(jnp.arange(n_tiles, dtype=jnp.int32)[:, None] >= tile_end[None, :], axis=1),
        N_GROUPS - 1).astype(jnp.int32)
    used = tile_end[N_GROUPS - 1:].astype(jnp.int32)
    pad_start = (first_row + counts).astype(jnp.int32)
    next_start = jnp.concatenate([first_row[1:], jnp.full((1,), n_sorted, jnp.int32)])
    pad_len = (next_start - pad_start).astype(jnp.int32)

    moved = pltpu.CompilerParams(dimension_semantics=("arbitrary",), vmem_limit_bytes=VMEM_LIMIT)
    hs = pl.pallas_call(
        functools.partial(_scatter_kernel, rows=MOVE_TILE),
        grid_spec=pltpu.PrefetchScalarGridSpec(
            num_scalar_prefetch=2,
            grid=(n // MOVE_TILE,),
            in_specs=[pl.BlockSpec((MOVE_TILE, ROW_WIDTH), lambda i, ps, pn: (i, 0)),
                      pl.BlockSpec((1, 1, MOVE_TILE), lambda i, ps, pn: (i, 0, 0),
                                   memory_space=pltpu.SMEM)],
            out_specs=pl.BlockSpec(memory_space=pl.ANY),
            scratch_shapes=[pltpu.VMEM((SUBLANES, ROW_WIDTH), F32), pltpu.SemaphoreType.DMA(())]),
        out_shape=jax.ShapeDtypeStruct((n_sorted, ROW_WIDTH), F32),
        compiler_params=moved,
        name="moe_scatter",
    )(pad_start, pad_len, hc, dst)

    group_w = lambda t, grp, use: (grp[t], 0, 0)
    ys = pl.pallas_call(
        functools.partial(_grouped_kernel, alpha=alpha),
        grid_spec=pltpu.PrefetchScalarGridSpec(
            num_scalar_prefetch=2,
            grid=(n_tiles,),
            in_specs=[pl.BlockSpec((MOE_TILE, ROW_WIDTH), lambda t, grp, use: (t, 0)),
                      pl.BlockSpec((1, D_MODEL, D_MODEL), group_w),
                      pl.BlockSpec((1, D_MODEL, D_MODEL), group_w),
                      pl.BlockSpec((1, D_MODEL, D_MODEL), group_w),
                      pl.BlockSpec((1, D_MODEL), lambda t, grp, use: (0, 0)),
                      pl.BlockSpec((1, D_MODEL), lambda t, grp, use: (0, 0))],
            out_specs=pl.BlockSpec((MOE_TILE, D_MODEL), lambda t, grp, use: (t, 0))),
        out_shape=jax.ShapeDtypeStruct((n_sorted, D_MODEL), F32),
        compiler_params=moved,
        name="moe_grouped",
    )(tile_group, used, hs, lw["w_group_gate"], lw["w_group_up"], lw["w_group_down"],
      lw["ln2_g"], lw["ln2_b"])

    return pl.pallas_call(
        functools.partial(_gather_kernel, rows=MOVE_TILE),
        grid=(n // MOVE_TILE,),
        in_specs=[pl.BlockSpec(memory_space=pl.ANY),
                  pl.BlockSpec((1, 1, MOVE_TILE), lambda i: (i, 0, 0), memory_space=pltpu.SMEM)],
        out_specs=pl.BlockSpec((MOVE_TILE, D_MODEL), lambda i: (i, 0)),
        out_shape=jax.ShapeDtypeStruct((n, D_MODEL), F32),
        scratch_shapes=[pltpu.SemaphoreType.DMA(())],
        compiler_params=moved,
        name="moe_gather",
    )(ys, dst)


def _state_layout(feat_major, tlen):
    bsz = feat_major.shape[0]
    return feat_major.reshape(bsz, SB_HEADS, SB_HEAD_DIM, tlen).transpose(0, 3, 1, 2)


def _feature_major(state):
    bsz, tlen = state.shape[:2]
    return state.transpose(0, 2, 3, 1).reshape(bsz, SB_DIM, tlen)


def _encoder_layer(x, k_hist, v_hist, conv_hist, lw, *, alpha, proj_tm, post_tm, moe_tm, tq):
    bsz, tlen, _ = x.shape
    n = bsz * tlen
    x2d = x.reshape(n, D_MODEL)
    q, k_t, v_t, kb_t, vb_t, u = _proj(x2d, lw["w_in"], lw["w_kv_t"], bsz=bsz, tlen=tlen, tm=proj_tm)

    nq = tlen // tq
    if k_hist is None:
        kn, vn, ck, cv = kb_t, vb_t, None, None
    else:
        pad = lambda a: jnp.pad(a, ((0, 0), (0, 0), (0, KEY_TILE - tlen)))
        kn, vn, ck, cv = pad(kb_t), pad(vb_t), _feature_major(k_hist), _feature_major(v_hist)
    attn = _attention(q, kn, vn, ck, cv, bsz=bsz, tq=tq, nq=nq)

    if conv_hist is None:
        hist = jnp.zeros((bsz * HALO, CONV_DIM), F32)
    else:
        hist = jnp.pad(conv_hist, ((0, 0), (HALO - (CONV_K - 1), 0), (0, 0))).reshape(bsz * HALO, CONV_DIM)
    hc, counts = _post(x2d, attn, u, hist, lw, bsz=bsz, tlen=tlen, tm=post_tm, alpha=alpha)
    if n % MOVE_TILE == 0:
        y = _moe_sparse(hc, counts[0, :N_GROUPS], lw, alpha=alpha)
    else:
        y = _moe_dense(hc, lw, tm=moe_tm, alpha=alpha)

    u3 = u.reshape(bsz, tlen, CONV_DIM)
    if conv_hist is None:
        conv_state = u3[:, tlen - (CONV_K - 1):]
    else:
        conv_state = jnp.concatenate([conv_hist, u3], axis=1)[:, -(CONV_K - 1):]
    return y.reshape(bsz, tlen, D_MODEL), _state_layout(k_t, tlen), _state_layout(v_t, tlen), conv_state


def _by_group_cols(w):
    w = w.astype(BF16).reshape(N_GROUPS, EXPERTS_PER_GROUP, D_MODEL, EXPERT_DFF)
    return w.transpose(0, 2, 1, 3).reshape(N_GROUPS, D_MODEL, EXPERTS_PER_GROUP * EXPERT_DFF)


def _layer_weights(l, w_in, w_branch_attn, w_branch_conv, conv_w, conv_b, conv_ln_g, conv_ln_b,
                   w_gate, b_gate, w_out, ln1_g, ln1_b, w_router_group, b_router_group,
                   w_router_expert, b_router_expert, w_expert_gate, w_expert_up, w_expert_down,
                   ln2_g, ln2_b):
    n_route = N_GROUPS + N_EXPERTS
    w_router = jnp.concatenate([w_router_group[l], w_router_expert[l]], axis=1)
    w_router = jnp.pad(w_router, ((0, 0), (0, ROUTER_LANES - n_route)))
    b_router = jnp.pad(jnp.concatenate([b_router_group[l], b_router_expert[l]]),
                       (0, ROUTER_LANES - n_route))
    row = lambda a: a.reshape(1, -1).astype(F32)
    return dict(
        w_in=w_in[l].astype(BF16), w_kv_t=w_in[l][:, SB_DIM:3 * SB_DIM].T.astype(BF16),
        wba=w_branch_attn[l].astype(BF16),
        wbc=w_branch_conv[l].astype(BF16),
        conv_w=jnp.broadcast_to(conv_w[l].astype(F32)[:, None, :], (CONV_K, SUBLANES, CONV_DIM)),
        conv_b=row(conv_b[l]),
        conv_ln_g=row(conv_ln_g[l]), conv_ln_b=row(conv_ln_b[l]),
        w_gate=w_gate[l].astype(BF16), b_gate=row(b_gate[l]), w_out=w_out[l].astype(BF16),
        ln1_g=row(ln1_g[l]), ln1_b=row(ln1_b[l]),
        w_router=w_router.astype(BF16), b_router=row(b_router),
        w_expert_gate=w_expert_gate[l].astype(BF16), w_expert_up=w_expert_up[l].astype(BF16),
        w_expert_down=w_expert_down[l].astype(BF16), ln2_g=row(ln2_g[l]), ln2_b=row(ln2_b[l]),
        w_group_gate=_by_group_cols(w_expert_gate[l]), w_group_up=_by_group_cols(w_expert_up[l]),
        w_group_down=w_expert_down[l].astype(BF16).reshape(
            N_GROUPS, EXPERTS_PER_GROUP * EXPERT_DFF, D_MODEL))


def kernel(x_prompt, x_sample, cache_k, cache_v, state_conv, w_in, w_branch_attn, w_branch_conv, conv_w, conv_b, conv_ln_g, conv_ln_b, w_gate, b_gate, w_out, ln1_g, ln1_b, w_router_group, b_router_group, w_router_expert, b_router_expert, w_expert_gate, w_expert_up, w_expert_down, ln2_g, ln2_b):
    depth = w_in.shape[0]
    alpha = (2.0 * depth) ** 0.25
    h_p, h_s = x_prompt, x_sample
    outs = [[] for _ in range(6)]
    for l in range(depth):
        lw = _layer_weights(l, w_in, w_branch_attn, w_branch_conv, conv_w, conv_b, conv_ln_g,
                            conv_ln_b, w_gate, b_gate, w_out, ln1_g, ln1_b, w_router_group,
                            b_router_group, w_router_expert, b_router_expert, w_expert_gate,
                            w_expert_up, w_expert_down, ln2_g, ln2_b)
        n_p = h_p.shape[0] * h_p.shape[1]
        h_p, k_p, v_p, c_p = _encoder_layer(
            h_p, None, None, None, lw, alpha=alpha,
            proj_tm=min(512, h_p.shape[1]), post_tm=min(512, h_p.shape[1]), moe_tm=min(1024, n_p),
            tq=128)
        ts = h_s.shape[1]
        h_s, k_s, v_s, c_s = _encoder_layer(
            h_s, cache_k[l], cache_v[l], state_conv[l], lw, alpha=alpha,
            proj_tm=ts, post_tm=ts, moe_tm=h_s.shape[0] * ts, tq=ts)
        for lst, val in zip(outs, (k_p, v_p, c_p, k_s, v_s, c_s)):
            lst.append(val)
    return (h_p, h_s) + tuple(jnp.stack(o) for o in outs)
```

```python
import functools

import jax
import jax.numpy as jnp
from jax import lax
from jax.experimental import pallas as pl
from jax.experimental.pallas import tpu as pltpu

F32 = jnp.float32
BF16 = jnp.bfloat16

D_MODEL = 1024
SB_DIM = 512
SB_HEADS = 8
SB_HEAD_DIM = 64
SB_SCALE = SB_HEAD_DIM ** -0.5
CONV_DIM = 512
CONV_K = 31
N_GROUPS = 4
EXPERTS_PER_GROUP = 4
N_EXPERTS = 16
EXPERT_DFF = 256
LN_EPS = 1e-5
IN_DIM = 3 * SB_DIM + 2 * CONV_DIM

LANES = 128
SUBLANES = 8
CONV_CHUNK = 32
MXU_COLS = 256
KEY_TILE = 128
CACHE_WINDOW_TILES = 4
HALO = 32
ROUTER_LANES = 128
ROW_WIDTH = D_MODEL + ROUTER_LANES
MOE_TILE = 256
MOVE_TILE = 2048
VMEM_LIMIT = 56 * 1024 * 1024
EXP_UNDERFLOW = -88.0
MASKED_LOG_WEIGHT = -1e30


def _layer_norm(x, g, b):
    mu = jnp.mean(x, axis=-1, keepdims=True)
    xc = x - mu
    var = jnp.mean(xc * xc, axis=-1, keepdims=True)
    return xc * lax.rsqrt(var + LN_EPS) * g + b


def _proj_kernel(x_ref, w_ref, wkv_t_ref, q_ref, kt_ref, vt_ref, ktb_ref, vtb_ref, u_ref):
    xb = x_ref[...].astype(BF16)

    def mm(lo, hi):
        return jnp.dot(xb, w_ref[:, lo:hi], preferred_element_type=F32)

    q_ref[...] = (mm(0, SB_DIM) * SB_SCALE).astype(BF16)
    kv_t = lax.dot_general(wkv_t_ref[...], xb, (((1,), (1,)), ((), ())), preferred_element_type=F32)
    kt_ref[0] = kv_t[:SB_DIM]
    ktb_ref[0] = kv_t[:SB_DIM].astype(BF16)
    vt_ref[0] = kv_t[SB_DIM:]
    vtb_ref[0] = kv_t[SB_DIM:].astype(BF16)
    val = mm(3 * SB_DIM, 3 * SB_DIM + CONV_DIM)
    gate = mm(3 * SB_DIM + CONV_DIM, IN_DIM)
    u_ref[...] = val * jax.nn.sigmoid(gate)


def _proj(x2d, w_in_bf, wkv_t_bf, *, bsz, tlen, tm):
    n = x2d.shape[0]
    nt = tlen // tm
    row = lambda i: (i, 0)
    feat = lambda i: (i // nt, 0, i % nt)
    outs = [jax.ShapeDtypeStruct((n, SB_DIM), BF16),
            jax.ShapeDtypeStruct((bsz, SB_DIM, tlen), F32),
            jax.ShapeDtypeStruct((bsz, SB_DIM, tlen), F32),
            jax.ShapeDtypeStruct((bsz, SB_DIM, tlen), BF16),
            jax.ShapeDtypeStruct((bsz, SB_DIM, tlen), BF16),
            jax.ShapeDtypeStruct((n, CONV_DIM), F32)]
    return pl.pallas_call(
        _proj_kernel,
        grid=(n // tm,),
        in_specs=[pl.BlockSpec((tm, D_MODEL), row),
                  pl.BlockSpec((D_MODEL, IN_DIM), lambda i: (0, 0)),
                  pl.BlockSpec((2 * SB_DIM, D_MODEL), lambda i: (0, 0))],
        out_specs=([pl.BlockSpec((tm, SB_DIM), row)] + [pl.BlockSpec((1, SB_DIM, tm), feat)] * 4
                   + [pl.BlockSpec((tm, CONV_DIM), row)]),
        out_shape=outs,
        compiler_params=pltpu.CompilerParams(dimension_semantics=("arbitrary",),
                                             vmem_limit_bytes=VMEM_LIMIT),
        name="proj",
    )(x2d, w_in_bf, wkv_t_bf)


def _attn_kernel(*refs, tq, pairs, n_cache_tiles, window_tiles):
    if n_cache_tiles:
        (q_ref, kn_ref, vn_ref, kwin_ref, vwin_ref, kc_hbm, vc_hbm, uu_ref, o_ref,
         qs_ref, acc_ref, run_ref, zs_ref, vs_ref, kst_ref, vst_ref, kdeep_ref, vdeep_ref,
         sem) = refs
    else:
        q_ref, kn_ref, vn_ref, uu_ref, o_ref, qs_ref, acc_ref, run_ref, zs_ref, vs_ref = refs
    i = pl.program_id(1)
    rows = 2 * tq
    uu = uu_ref[...]

    lane_q = lax.broadcasted_iota(jnp.int32, (tq, LANES), 1)
    feat_k = lax.broadcasted_iota(jnp.int32, (LANES, KEY_TILE), 0)
    r_idx = lax.broadcasted_iota(jnp.int32, (rows, KEY_TILE), 0)
    r_idx = jnp.where(r_idx >= tq, r_idx - tq, r_idx)
    c_idx = lax.broadcasted_iota(jnp.int32, (rows, KEY_TILE), 1)
    causal = c_idx < r_idx

    def score_sweep(slot, load_kv, mask, first):
        zs, splits = [], []
        for p in range(pairs):
            kt, vt = load_kv(slice(p * LANES, (p + 1) * LANES))
            vs_ref[slot, p, 0] = jnp.where(feat_k < SB_HEAD_DIM, vt, jnp.zeros_like(vt))
            vs_ref[slot, p, 1] = jnp.where(feat_k >= SB_HEAD_DIM, vt, jnp.zeros_like(vt))
            z = jnp.dot(qs_ref[p], kt, preferred_element_type=F32)
            lk = -(jnp.maximum(z, 0.0) + jnp.log(1.0 + jnp.exp(-jnp.abs(z))))
            if mask is not None:
                lk = jnp.where(mask, lk, 0.0)
            hi = lk.astype(BF16)
            lo = (lk - hi.astype(F32)).astype(BF16)
            zs.append(z)
            splits.append(jnp.concatenate([hi, lo], axis=1))
        su = jnp.dot(jnp.concatenate(splits, axis=0), uu, preferred_element_type=F32)
        worst = None
        for p in range(pairs):
            sp = su[p * rows:(p + 1) * rows]
            logw = zs[p] + sp[:, :KEY_TILE]
            tot = sp[:, KEY_TILE:]
            if not first:
                run = run_ref[p]
                logw = logw + run
                tot = tot + run
            if mask is not None:
                logw = jnp.where(mask, logw, MASKED_LOG_WEIGHT)
            zs_ref[slot, p] = logw
            run_ref[p] = tot
            worst = tot if worst is None else jnp.maximum(worst, tot)
        return jnp.max(worst)

    def value_sweep(slot):
        over_keys = (((1,), (1,)), ((), ()))
        for p in range(pairs):
            wb = jnp.exp(zs_ref[slot, p]).astype(BF16)
            acc_ref[p] += (
                lax.dot_general(wb[:tq], vs_ref[slot, p, 0], over_keys, preferred_element_type=F32)
                + lax.dot_general(wb[tq:], vs_ref[slot, p, 1], over_keys, preferred_element_type=F32))

    for p in range(pairs):
        qp = q_ref[:, p * LANES:(p + 1) * LANES]
        qs_ref[p] = jnp.concatenate(
            [jnp.where(lane_q < SB_HEAD_DIM, qp, jnp.zeros_like(qp)),
             jnp.where(lane_q >= SB_HEAD_DIM, qp, jnp.zeros_like(qp))], axis=0)

    def new_tile(j):
        keys = pl.ds(pl.multiple_of(j * KEY_TILE, KEY_TILE), KEY_TILE)
        return lambda feats: (kn_ref[0, feats, keys], vn_ref[0, feats, keys])

    def cache_tile(j):
        first_window_tile = n_cache_tiles - window_tiles

        @pl.when(j >= first_window_tile)
        def _():
            keys = pl.ds(pl.multiple_of((j - first_window_tile) * KEY_TILE, KEY_TILE), KEY_TILE)
            kst_ref[...] = kwin_ref[0, :, keys].astype(BF16)
            vst_ref[...] = vwin_ref[0, :, keys].astype(BF16)

        @pl.when(j < first_window_tile)
        def _():
            b = pl.program_id(0)
            keys = pl.ds(pl.multiple_of(j * KEY_TILE, KEY_TILE), KEY_TILE)
            fetch_k = pltpu.make_async_copy(kc_hbm.at[b, :, keys], kdeep_ref, sem.at[0])
            fetch_v = pltpu.make_async_copy(vc_hbm.at[b, :, keys], vdeep_ref, sem.at[1])
            fetch_k.start()
            fetch_v.start()
            fetch_k.wait()
            fetch_v.wait()
            kst_ref[...] = kdeep_ref[...].astype(BF16)
            vst_ref[...] = vdeep_ref[...].astype(BF16)

        return lambda feats: (kst_ref[feats, :], vst_ref[feats, :])

    live = lambda c: (c[0] >= 0) & (c[2] > EXP_UNDERFLOW)

    def pipelined(tile_of):
        def body(c):
            j, slot, _ = c
            value_sweep(slot)
            return j - 1, 1 - slot, score_sweep(1 - slot, tile_of(j), None, False)
        return body

    worst = score_sweep(0, new_tile(i), causal, True)
    acc_ref[...] = jnp.zeros_like(acc_ref)
    carry = lax.while_loop(live, pipelined(new_tile), (i - 1, jnp.int32(0), worst))
    if n_cache_tiles:
        carry = lax.while_loop(live, pipelined(cache_tile),
                               (jnp.int32(n_cache_tiles - 1), carry[1], carry[2]))
    value_sweep(carry[1])

    for p in range(pairs):
        o_ref[:, p * LANES:(p + 1) * LANES] = acc_ref[p].astype(BF16)


def _suffix_sum_matrix():
    r = lax.broadcasted_iota(jnp.int32, (2 * KEY_TILE, 2 * KEY_TILE), 0) % KEY_TILE
    c = lax.broadcasted_iota(jnp.int32, (2 * KEY_TILE, 2 * KEY_TILE), 1)
    return jnp.where((c >= KEY_TILE) | (r >= c), 1.0, 0.0).astype(BF16)


def _attention(q, kn, vn, cache_k, cache_v, *, bsz, tq, nq):
    n = q.shape[0]
    pairs = SB_DIM // LANES
    new_cols = kn.shape[2]
    n_cache_tiles = 0 if cache_k is None else cache_k.shape[2] // KEY_TILE
    window_tiles = 0
    in_specs = [pl.BlockSpec((tq, SB_DIM), lambda b, i: (b * nq + i, 0)),
                pl.BlockSpec((1, SB_DIM, new_cols), lambda b, i: (b, 0, 0)),
                pl.BlockSpec((1, SB_DIM, new_cols), lambda b, i: (b, 0, 0))]
    args = [q, kn, vn]
    scratch = [pltpu.VMEM((pairs, 2 * tq, LANES), BF16),
               pltpu.VMEM((pairs, tq, LANES), F32),
               pltpu.VMEM((pairs, 2 * tq, KEY_TILE), F32),
               pltpu.VMEM((2, pairs, 2 * tq, KEY_TILE), F32),
               pltpu.VMEM((2, pairs, 2, LANES, KEY_TILE), BF16)]
    if n_cache_tiles:
        assert cache_k.shape[2] % KEY_TILE == 0
        window_tiles = CACHE_WINDOW_TILES if n_cache_tiles % CACHE_WINDOW_TILES == 0 else 1
        window = (1, SB_DIM, window_tiles * KEY_TILE)
        last = n_cache_tiles // window_tiles - 1
        in_specs += [pl.BlockSpec(window, lambda b, i: (b, 0, last))] * 2
        in_specs += [pl.BlockSpec(memory_space=pl.ANY)] * 2
        args += [cache_k, cache_v, cache_k, cache_v]
        scratch += [pltpu.VMEM((SB_DIM, KEY_TILE), BF16),
                    pltpu.VMEM((SB_DIM, KEY_TILE), BF16),
                    pltpu.VMEM((SB_DIM, KEY_TILE), F32),
                    pltpu.VMEM((SB_DIM, KEY_TILE), F32),
                    pltpu.SemaphoreType.DMA((2,))]
    in_specs.append(pl.BlockSpec((2 * KEY_TILE, 2 * KEY_TILE), lambda b, i: (0, 0)))
    args.append(_suffix_sum_matrix())
    return pl.pallas_call(
        functools.partial(_attn_kernel, tq=tq, pairs=pairs, n_cache_tiles=n_cache_tiles,
                          window_tiles=window_tiles),
        grid=(bsz, nq),
        in_specs=in_specs,
        out_specs=pl.BlockSpec((tq, SB_DIM), lambda b, i: (b * nq + i, 0)),
        out_shape=jax.ShapeDtypeStruct((n, SB_DIM), BF16),
        scratch_shapes=scratch,
        compiler_params=pltpu.CompilerParams(
            dimension_semantics=("arbitrary", "arbitrary"),
            vmem_limit_bytes=VMEM_LIMIT),
        name="attn",
    )(*args)


def _post_kernel(x_ref, attn_ref, u_ref, halo_ref, hist_ref, wba_ref, wbc_ref, cw_ref, cb_ref,
                 clg_ref, clb_ref, wg_ref, bg_ref, wo_ref, l1g_ref, l1b_ref, wr_ref, br_ref,
                 lt_ref, hc_ref, cnt_ref, ext_ref, sh_ref, conv_ref, xb_ref, gates_ref, ba_ref,
                 *, tm, alpha):
    i = pl.program_id(1)
    prev = jnp.where(i == 0, hist_ref[...], halo_ref[...])
    ext_ref[0:HALO, :] = prev
    ext_ref[HALO:HALO + tm, :] = u_ref[...]
    span = tm + HALO - SUBLANES
    for s in range(1, SUBLANES):
        sh_ref[s - 1] = ext_ref[s:s + span, :]
    xb_ref[...] = x_ref[...].astype(BF16)

    def gate_block(n):
        cols = slice(n * MXU_COLS, (n + 1) * MXU_COLS)
        gates_ref[:, cols] = jax.nn.sigmoid(
            jnp.dot(xb_ref[...], wg_ref[:, cols], preferred_element_type=F32) + bg_ref[:, cols])

    def attn_block(n):
        cols = slice(n * MXU_COLS, (n + 1) * MXU_COLS)
        ba_ref[:, cols] = jnp.dot(attn_ref[...], wba_ref[:, cols], preferred_element_type=F32)

    jobs = ([functools.partial(gate_block, n) for n in range(2 * D_MODEL // MXU_COLS)]
            + [functools.partial(attn_block, n) for n in range(D_MODEL // MXU_COLS)])

    first = HALO - (CONV_K - 1)
    chunk = min(CONV_CHUNK, tm)
    tiles = chunk // SUBLANES
    for idx, r0 in enumerate(range(0, tm, chunk)):
        acc = jnp.zeros((tiles, SUBLANES, CONV_DIM), F32) + cb_ref[...]
        for j in range(CONV_K):
            phase = (first + j) % SUBLANES
            base = first + j - phase + r0
            rows = ext_ref[base:base + chunk, :] if phase == 0 else sh_ref[phase - 1, base:base + chunk, :]
            acc = acc + rows.reshape(tiles, SUBLANES, CONV_DIM) * cw_ref[j]
        conv_ref[r0:r0 + chunk, :] = acc.reshape(chunk, CONV_DIM)
        if jobs:
            jobs.pop(0)()
    while jobs:
        jobs.pop(0)()

    c = _layer_norm(conv_ref[...], clg_ref[...], clb_ref[...])
    c = c * jax.nn.sigmoid(c)
    branch_b = jnp.dot(c.astype(BF16), wbc_ref[...], preferred_element_type=F32)
    x = x_ref[...]
    m = gates_ref[:, :D_MODEL] * ba_ref[...] + gates_ref[:, D_MODEL:] * branch_b
    mixed = jnp.dot(m.astype(BF16), wo_ref[...], preferred_element_type=F32)
    h = _layer_norm(alpha * x + mixed, l1g_ref[...], l1b_ref[...])
    hc_ref[:, :D_MODEL] = h

    logits = jnp.dot(h.astype(BF16), wr_ref[...], preferred_element_type=F32) + br_ref[...]
    lane = lax.broadcasted_iota(jnp.int32, (tm, ROUTER_LANES), 1)
    neg = jnp.float32(-jnp.inf)
    gl = jnp.where(lane < N_GROUPS, logits, neg)
    gmax = jnp.max(gl, axis=-1, keepdims=True)
    g_idx = jnp.min(jnp.where(gl == gmax, lane, ROUTER_LANES), axis=-1, keepdims=True)
    p_group = 1.0 / jnp.sum(jnp.exp(gl - gmax), axis=-1, keepdims=True)
    lo = N_GROUPS + EXPERTS_PER_GROUP * g_idx
    el = jnp.where((lane >= lo) & (lane < lo + EXPERTS_PER_GROUP), logits, neg)
    top1 = jnp.max(el, axis=-1, keepdims=True)
    i1 = jnp.min(jnp.where(el == top1, lane, ROUTER_LANES), axis=-1, keepdims=True)
    el2 = jnp.where(lane == i1, neg, el)
    top2 = jnp.max(el2, axis=-1, keepdims=True)
    i2 = jnp.min(jnp.where(el2 == top2, lane, ROUTER_LANES), axis=-1, keepdims=True)
    e2 = jnp.exp(top2 - top1)
    p1 = 1.0 / (1.0 + e2)
    p2 = e2 / (1.0 + e2)
    comb = (jnp.where(lane == i1, p_group * p1, 0.0)
            + jnp.where(lane == i2, p_group * p2, 0.0))

    @pl.when((pl.program_id(0) == 0) & (i == 0))
    def _():
        cnt_ref[...] = jnp.zeros_like(cnt_ref)

    member = (lane == g_idx).astype(F32)
    before = jnp.dot(lt_ref[...], member.astype(BF16), preferred_element_type=F32) + cnt_ref[0:1, :]
    rank = jnp.sum(member * before, axis=-1, keepdims=True)
    cnt_ref[...] = cnt_ref[...] + jnp.sum(member, axis=0, keepdims=True)
    hc_ref[:, D_MODEL:] = jnp.where(lane == 0, g_idx.astype(F32), jnp.where(lane == 1, rank, comb))


def _post(x2d, attn, u, hist, lw, *, bsz, tlen, tm, alpha):
    n = x2d.shape[0]
    nt = tlen // tm
    row = lambda b, i: (b * nt + i, 0)
    const = lambda b, i: (0, 0)

    def halo_map(b, i):
        return (jnp.maximum((b * tlen + i * tm) // HALO - 1, 0), 0)

    full = lambda a: pl.BlockSpec(a.shape, lambda b, i, nd=a.ndim: (0,) * nd)
    r = lax.broadcasted_iota(jnp.int32, (tm, tm), 0)
    c = lax.broadcasted_iota(jnp.int32, (tm, tm), 1)
    earlier = jnp.where(c < r, 1.0, 0.0).astype(BF16)
    weights = [lw["wba"], lw["wbc"], lw["conv_w"], lw["conv_b"], lw["conv_ln_g"], lw["conv_ln_b"],
               lw["w_gate"], lw["b_gate"], lw["w_out"], lw["ln1_g"], lw["ln1_b"],
               lw["w_router"], lw["b_router"], earlier]
    return pl.pallas_call(
        functools.partial(_post_kernel, tm=tm, alpha=alpha),
        grid=(bsz, nt),
        in_specs=[pl.BlockSpec((tm, D_MODEL), row),
                  pl.BlockSpec((tm, SB_DIM), row),
                  pl.BlockSpec((tm, CONV_DIM), row),
                  pl.BlockSpec((HALO, CONV_DIM), halo_map),
                  pl.BlockSpec((HALO, CONV_DIM), lambda b, i: (b, 0))] + [full(w) for w in weights],
        out_specs=[pl.BlockSpec((tm, ROW_WIDTH), row),
                   pl.BlockSpec((SUBLANES, ROUTER_LANES), lambda b, i: (0, 0))],
        out_shape=[jax.ShapeDtypeStruct((n, ROW_WIDTH), F32),
                   jax.ShapeDtypeStruct((SUBLANES, ROUTER_LANES), F32)],
        scratch_shapes=[pltpu.VMEM((HALO + tm, CONV_DIM), F32),
                        pltpu.VMEM((SUBLANES - 1, HALO + tm - SUBLANES, CONV_DIM), F32),
                        pltpu.VMEM((tm, CONV_DIM), F32),
                        pltpu.VMEM((tm, D_MODEL), BF16),
                        pltpu.VMEM((tm, 2 * D_MODEL), F32),
                        pltpu.VMEM((tm, D_MODEL), F32)],
        compiler_params=pltpu.CompilerParams(dimension_semantics=("arbitrary", "arbitrary"),
                                             vmem_limit_bytes=VMEM_LIMIT),
        name="post",
    )(x2d, attn, u, u, hist, *weights)


def _combine_weight(meta, lane_id):
    lane = lax.broadcasted_iota(jnp.int32, meta.shape, 1)
    return jnp.sum(jnp.where(lane == lane_id, meta, 0.0), axis=-1, keepdims=True)


def _moe_dense_kernel(hc_ref, wg_ref, wu_ref, wd_ref, l2g_ref, l2b_ref, y_ref,
                      hb_ref, acc_ref, *, alpha):
    e = pl.program_id(1)

    @pl.when(e == 0)
    def _():
        hb_ref[...] = hc_ref[:, :D_MODEL].astype(BF16)
        acc_ref[...] = jnp.zeros_like(acc_ref)

    hb = hb_ref[...]
    gate = jnp.dot(hb, wg_ref[0], preferred_element_type=F32)
    up = jnp.dot(hb, wu_ref[0], preferred_element_type=F32)
    c_e = _combine_weight(hc_ref[:, D_MODEL:], N_GROUPS + e)
    hid = gate * jax.nn.sigmoid(gate) * up * c_e
    acc_ref[...] += jnp.dot(hid.astype(BF16), wd_ref[0], preferred_element_type=F32)

    @pl.when(e == N_EXPERTS - 1)
    def _():
        y_ref[...] = _layer_norm(alpha * hc_ref[:, :D_MODEL] + acc_ref[...],
                                 l2g_ref[...], l2b_ref[...])


def _moe_dense(hc, lw, *, tm, alpha):
    n = hc.shape[0]
    row = lambda i, e: (i, 0)
    const = lambda i, e: (0, 0)
    expert = lambda i, e: (e, 0, 0)
    return pl.pallas_call(
        functools.partial(_moe_dense_kernel, alpha=alpha),
        grid=(n // tm, N_EXPERTS),
        in_specs=[pl.BlockSpec((tm, ROW_WIDTH), row),
                  pl.BlockSpec((1, D_MODEL, EXPERT_DFF), expert),
                  pl.BlockSpec((1, D_MODEL, EXPERT_DFF), expert),
                  pl.BlockSpec((1, EXPERT_DFF, D_MODEL), expert),
                  pl.BlockSpec((1, D_MODEL), const),
                  pl.BlockSpec((1, D_MODEL), const)],
        out_specs=pl.BlockSpec((tm, D_MODEL), row),
        out_shape=jax.ShapeDtypeStruct((n, D_MODEL), F32),
        scratch_shapes=[pltpu.VMEM((tm, D_MODEL), BF16), pltpu.VMEM((tm, D_MODEL), F32)],
        compiler_params=pltpu.CompilerParams(dimension_semantics=("arbitrary", "arbitrary"),
                                             vmem_limit_bytes=VMEM_LIMIT),
        name="moe_dense",
    )(hc, lw["w_expert_gate"], lw["w_expert_up"], lw["w_expert_down"], lw["ln2_g"], lw["ln2_b"])


def _row_copy(src_ref, src_row, dst_ref, dst_row, sem):
    return pltpu.make_async_copy(src_ref.at[pl.ds(src_row, 1)], dst_ref.at[pl.ds(dst_row, 1)], sem)


def _for_each_row(rows, fn):
    def step(t, _):
        base = pl.multiple_of(t * SUBLANES, SUBLANES)
        for k in range(SUBLANES):
            fn(base + k, k)
        return 0

    lax.fori_loop(0, rows // SUBLANES, step, 0)


def _scatter_kernel(pad_start_ref, pad_len_ref, hc_ref, dst_ref, hs_ref, zero_ref, sem, *, rows):
    @pl.when(pl.program_id(0) == 0)
    def _():
        zero_ref[...] = jnp.zeros_like(zero_ref)
        for g in range(N_GROUPS):
            def fill(k, _):
                _row_copy(zero_ref, 0, hs_ref, pad_start_ref[g] + k, sem).start()
                return 0

            def drain(k, _):
                _row_copy(zero_ref, 0, hs_ref, pad_start_ref[g] + k, sem).wait()
                return 0

            lax.fori_loop(0, pad_len_ref[g], fill, 0)
            lax.fori_loop(0, pad_len_ref[g], drain, 0)

    _for_each_row(rows, lambda r, k: _row_copy(hc_ref, r, hs_ref, dst_ref[0, 0, r], sem).start(k % 2))
    _for_each_row(rows, lambda r, k: _row_copy(hc_ref, r, hs_ref, dst_ref[0, 0, r], sem).wait())


def _gather_kernel(ys_ref, dst_ref, y_ref, sem, *, rows):
    _for_each_row(rows, lambda r, k: _row_copy(ys_ref, dst_ref[0, 0, r], y_ref, r, sem).start(k % 2))
    _for_each_row(rows, lambda r, k: _row_copy(ys_ref, dst_ref[0, 0, r], y_ref, r, sem).wait())


def _grouped_kernel(group_ref, used_ref, hs_ref, wg_ref, wu_ref, wd_ref, l2g_ref, l2b_ref, ys_ref,
                    *, alpha):
    t = pl.program_id(0)

    @pl.when(t < used_ref[0])
    def _():
        h = hs_ref[:, :D_MODEL]
        meta = hs_ref[:, D_MODEL:]
        hb = h.astype(BF16)
        gate = jnp.dot(hb, wg_ref[0], preferred_element_type=F32)
        up = jnp.dot(hb, wu_ref[0], preferred_element_type=F32)
        hid = gate * jax.nn.sigmoid(gate) * up
        first = N_GROUPS + EXPERTS_PER_GROUP * group_ref[t]
        scaled = [hid[:, e * EXPERT_DFF:(e + 1) * EXPERT_DFF] * _combine_weight(meta, first + e)
                  for e in range(EXPERTS_PER_GROUP)]
        ff = jnp.dot(jnp.concatenate(scaled, axis=1).astype(BF16), wd_ref[0],
                     preferred_element_type=F32)
        ys_ref[...] = _layer_norm(alpha * h + ff, l2g_ref[...], l2b_ref[...])

    @pl.when(t >= used_ref[0])
    def _():
        ys_ref[...] = jnp.zeros_like(ys_ref)


def _moe_sparse(hc, counts, lw, *, alpha):
    n = hc.shape[0]
    n_tiles = n // MOE_TILE + N_GROUPS
    n_sorted = n_tiles * MOE_TILE
    counts = counts.astype(jnp.int32)
    tiles = (counts + MOE_TILE - 1) // MOE_TILE
    tile_end = jnp.cumsum(tiles)
    first_row = (tile_end - tiles) * MOE_TILE
    group = hc[:, D_MODEL].astype(jnp.int32)
    rank = hc[:, D_MODEL + 1].astype(jnp.int32)
    dst = (first_row[group] + rank).reshape(n // MOVE_TILE, 1, MOVE_TILE)
    tile_group = jnp.minimum(
        jnp.sum(jnp.arange(n_tiles, dtype=jnp.int32)[:, None] >= tile_end[None, :], axis=1),
        N_GROUPS - 1).astype(jnp.int32)
    used = tile_end[N_GROUPS - 1:].astype(jnp.int32)
    pad_start = (first_row + counts).astype(jnp.int32)
    next_start = jnp.concatenate([first_row[1:], jnp.full((1,), n_sorted, jnp.int32)])
    pad_len = (next_start - pad_start).astype(jnp.int32)

    moved = pltpu.CompilerParams(dimension_semantics=("arbitrary",), vmem_limit_bytes=VMEM_LIMIT)
    hs = pl.pallas_call(
        functools.partial(_scatter_kernel, rows=MOVE_TILE),
        grid_spec=pltpu.PrefetchScalarGridSpec(
            num_scalar_prefetch=2,
            grid=(n // MOVE_TILE,),
            in_specs=[pl.BlockSpec((MOVE_TILE, ROW_WIDTH), lambda i, ps, pn: (i, 0)),
                      pl.BlockSpec((1, 1, MOVE_TILE), lambda i, ps, pn: (i, 0, 0),
                                   memory_space=pltpu.SMEM)],
            out_specs=pl.BlockSpec(memory_space=pl.ANY),
            scratch_shapes=[pltpu.VMEM((SUBLANES, ROW_WIDTH), F32), pltpu.SemaphoreType.DMA(())]),
        out_shape=jax.ShapeDtypeStruct((n_sorted, ROW_WIDTH), F32),
        compiler_params=moved,
        name="moe_scatter",
    )(pad_start, pad_len, hc, dst)

    group_w = lambda t, grp, use: (grp[t], 0, 0)
    ys = pl.pallas_call(
        functools.partial(_grouped_kernel, alpha=alpha),
        grid_spec=pltpu.PrefetchScalarGridSpec(
            num_scalar_prefetch=2,
            grid=(n_tiles,),
            in_specs=[pl.BlockSpec((MOE_TILE, ROW_WIDTH), lambda t, grp, use: (t, 0)),
                      pl.BlockSpec((1, D_MODEL, D_MODEL), group_w),
                      pl.BlockSpec((1, D_MODEL, D_MODEL), group_w),
                      pl.BlockSpec((1, D_MODEL, D_MODEL), group_w),
                      pl.BlockSpec((1, D_MODEL), lambda t, grp, use: (0, 0)),
                      pl.BlockSpec((1, D_MODEL), lambda t, grp, use: (0, 0))],
            out_specs=pl.BlockSpec((MOE_TILE, D_MODEL), lambda t, grp, use: (t, 0))),
        out_shape=jax.ShapeDtypeStruct((n_sorted, D_MODEL), F32),
        compiler_params=moved,
        name="moe_grouped",
    )(tile_group, used, hs, lw["w_group_gate"], lw["w_group_up"], lw["w_group_down"],
      lw["ln2_g"], lw["ln2_b"])

    return pl.pallas_call(
        functools.partial(_gather_kernel, rows=MOVE_TILE),
        grid=(n // MOVE_TILE,),
        in_specs=[pl.BlockSpec(memory_space=pl.ANY),
                  pl.BlockSpec((1, 1, MOVE_TILE), lambda i: (i, 0, 0), memory_space=pltpu.SMEM)],
        out_specs=pl.BlockSpec((MOVE_TILE, D_MODEL), lambda i: (i, 0)),
        out_shape=jax.ShapeDtypeStruct((n, D_MODEL), F32),
        scratch_shapes=[pltpu.SemaphoreType.DMA(())],
        compiler_params=moved,
        name="moe_gather",
    )(ys, dst)


def _state_layout(feat_major, tlen):
    bsz = feat_major.shape[0]
    return feat_major.reshape(bsz, SB_HEADS, SB_HEAD_DIM, tlen).transpose(0, 3, 1, 2)


def _feature_major(state):
    bsz, tlen = state.shape[:2]
    return state.transpose(0, 2, 3, 1).reshape(bsz, SB_DIM, tlen)


def _encoder_layer(x, k_hist, v_hist, conv_hist, lw, *, alpha, proj_tm, post_tm, moe_tm, tq):
    bsz, tlen, _ = x.shape
    n = bsz * tlen
    x2d = x.reshape(n, D_MODEL)
    q, k_t, v_t, kb_t, vb_t, u = _proj(x2d, lw["w_in"], lw["w_kv_t"], bsz=bsz, tlen=tlen, tm=proj_tm)

    nq = tlen // tq
    if k_hist is None:
        kn, vn, ck, cv = kb_t, vb_t, None, None
    else:
        pad = lambda a: jnp.pad(a, ((0, 0), (0, 0), (0, KEY_TILE - tlen)))
        kn, vn, ck, cv = pad(kb_t), pad(vb_t), _feature_major(k_hist), _feature_major(v_hist)
    attn = _attention(q, kn, vn, ck, cv, bsz=bsz, tq=tq, nq=nq)

    if conv_hist is None:
        hist = jnp.zeros((bsz * HALO, CONV_DIM), F32)
    else:
        hist = jnp.pad(conv_hist, ((0, 0), (HALO - (CONV_K - 1), 0), (0, 0))).reshape(bsz * HALO, CONV_DIM)
    hc, counts = _post(x2d, attn, u, hist, lw, bsz=bsz, tlen=tlen, tm=post_tm, alpha=alpha)
    if n % MOVE_TILE == 0:
        y = _moe_sparse(hc, counts[0, :N_GROUPS], lw, alpha=alpha)
    else:
        y = _moe_dense(hc, lw, tm=moe_tm, alpha=alpha)

    u3 = u.reshape(bsz, tlen, CONV_DIM)
    if conv_hist is None:
        conv_state = u3[:, tlen - (CONV_K - 1):]
    else:
        conv_state = jnp.concatenate([conv_hist, u3], axis=1)[:, -(CONV_K - 1):]
    return y.reshape(bsz, tlen, D_MODEL), _state_layout(k_t, tlen), _state_layout(v_t, tlen), conv_state


def _by_group_cols(w):
    w = w.astype(BF16).reshape(N_GROUPS, EXPERTS_PER_GROUP, D_MODEL, EXPERT_DFF)
    return w.transpose(0, 2, 1, 3).reshape(N_GROUPS, D_MODEL, EXPERTS_PER_GROUP * EXPERT_DFF)


def _layer_weights(l, w_in, w_branch_attn, w_branch_conv, conv_w, conv_b, conv_ln_g, conv_ln_b,
                   w_gate, b_gate, w_out, ln1_g, ln1_b, w_router_group, b_router_group,
                   w_router_expert, b_router_expert, w_expert_gate, w_expert_up, w_expert_down,
                   ln2_g, ln2_b):
    n_route = N_GROUPS + N_EXPERTS
    w_router = jnp.concatenate([w_router_group[l], w_router_expert[l]], axis=1)
    w_router = jnp.pad(w_router, ((0, 0), (0, ROUTER_LANES - n_route)))
    b_router = jnp.pad(jnp.concatenate([b_router_group[l], b_router_expert[l]]),
                       (0, ROUTER_LANES - n_route))
    row = lambda a: a.reshape(1, -1).astype(F32)
    return dict(
        w_in=w_in[l].astype(BF16), w_kv_t=w_in[l][:, SB_DIM:3 * SB_DIM].T.astype(BF16),
        wba=w_branch_attn[l].astype(BF16),
        wbc=w_branch_conv[l].astype(BF16),
        conv_w=jnp.broadcast_to(conv_w[l].astype(F32)[:, None, :], (CONV_K, SUBLANES, CONV_DIM)),
        conv_b=row(conv_b[l]),
        conv_ln_g=row(conv_ln_g[l]), conv_ln_b=row(conv_ln_b[l]),
        w_gate=w_gate[l].astype(BF16), b_gate=row(b_gate[l]), w_out=w_out[l].astype(BF16),
        ln1_g=row(ln1_g[l]), ln1_b=row(ln1_b[l]),
        w_router=w_router.astype(BF16), b_router=row(b_router),
        w_expert_gate=w_expert_gate[l].astype(BF16), w_expert_up=w_expert_up[l].astype(BF16),
        w_expert_down=w_expert_down[l].astype(BF16), ln2_g=row(ln2_g[l]), ln2_b=row(ln2_b[l]),
        w_group_gate=_by_group_cols(w_expert_gate[l]), w_group_up=_by_group_cols(w_expert_up[l]),
        w_group_down=w_expert_down[l].astype(BF16).reshape(
            N_GROUPS, EXPERTS_PER_GROUP * EXPERT_DFF, D_MODEL))


def kernel(x_prompt, x_sample, cache_k, cache_v, state_conv, w_in, w_branch_attn, w_branch_conv, conv_w, conv_b, conv_ln_g, conv_ln_b, w_gate, b_gate, w_out, ln1_g, ln1_b, w_router_group, b_router_group, w_router_expert, b_router_expert, w_expert_gate, w_expert_up, w_expert_down, ln2_g, ln2_b):
    depth = w_in.shape[0]
    alpha = (2.0 * depth) ** 0.25
    h_p, h_s = x_prompt, x_sample
    outs = [[] for _ in range(6)]
    for l in range(depth):
        lw = _layer_weights(l, w_in, w_branch_attn, w_branch_conv, conv_w, conv_b, conv_ln_g,
                            conv_ln_b, w_gate, b_gate, w_out, ln1_g, ln1_b, w_router_group,
                            b_router_group, w_router_expert, b_router_expert, w_expert_gate,
                            w_expert_up, w_expert_down, ln2_g, ln2_b)
        n_p = h_p.shape[0] * h_p.shape[1]
        h_p, k_p, v_p, c_p = _encoder_layer(
            h_p, None, None, None, lw, alpha=alpha,
            proj_tm=min(512, h_p.shape[1]), post_tm=min(512, h_p.shape[1]), moe_tm=min(1024, n_p),
            tq=128)
        ts = h_s.shape[1]
        h_s, k_s, v_s, c_s = _encoder_layer(
            h_s, cache_k[l], cache_v[l], state_conv[l], lw, alpha=alpha,
            proj_tm=ts, post_tm=ts, moe_tm=h_s.shape[0] * ts, tq=ts)
        for lst, val in zip(outs, (k_p, v_p, c_p, k_s, v_s, c_s)):
            lst.append(val)
    return (h_p, h_s) + tuple(jnp.stack(o) for o in outs)
```
